```python
import jax, jax.numpy as jnp
from jax import lax
import numpy as np

D_MODEL = 1024
BATCH = 2
SEQ = 16384
DEPTH = 2

CHUNK = 64
N_MEM = 256
HEAD_DIM = 64
FOX_HEADS = 8
CHUNK_HEADS = 8
FOX_WIDTH = FOX_HEADS * HEAD_DIM
CHUNK_WIDTH = CHUNK_HEADS * HEAD_DIM
IN_COLS = 3 * FOX_WIDTH + FOX_HEADS + 3 * CHUNK_WIDTH
Q_BLOCK = 128
LEFT_CHUNKS = 8
REL_CLIP = 128
NUM_REL = CHUNK + REL_CLIP
MEM_HEADS = 4
MEM_HEAD_DIM = D_MODEL // MEM_HEADS
D_FF = -(-8 * D_MODEL // 768) * 256
EPS = 1e-6
NEG_INF = -1e30

kernel_name = "hybrid_fox_chunkrel_memxattn_block"


def rms_norm(x, g):
    xf = x.astype(jnp.float32)
    y = xf * lax.rsqrt(jnp.mean(xf * xf, axis=-1, keepdims=True) + EPS)
    return (y * g.astype(jnp.float32)).astype(x.dtype)


def fox_attention(q, k, v, log_f):
    B, S, H, D = q.shape
    nb = S // Q_BLOCK
    scale = D ** -0.5
    c = lax.cumsum(log_f, axis=1)
    c_keys = jnp.transpose(c, (0, 2, 1))
    k_pos = jnp.arange(S)
    q_blocks = jnp.transpose(q.reshape(B, nb, Q_BLOCK, H, D), (1, 0, 2, 3, 4))
    c_blocks = jnp.transpose(c.reshape(B, nb, Q_BLOCK, H), (1, 0, 3, 2))

    def one_block(args):
        i, q_i, c_i = args
        q_pos = i * Q_BLOCK + jnp.arange(Q_BLOCK)
        s = jnp.einsum('bqhd,bkhd->bhqk', q_i, k).astype(jnp.float32) * scale
        s = s + c_i[..., :, None] - c_keys[:, :, None, :]
        s = jnp.where((k_pos[None, :] <= q_pos[:, None])[None, None], s, NEG_INF)
        p = jax.nn.softmax(s, axis=-1)
        return jnp.einsum('bhqk,bkhd->bqhd', p.astype(v.dtype), v)

    out = lax.map(one_block, (jnp.arange(nb), q_blocks, c_blocks))
    return jnp.transpose(out, (1, 0, 2, 3, 4)).reshape(B, S, H, D)


def chunked_rel_attention(q, k, v, rel_bias):
    B, S, H, D = q.shape
    nc = S // CHUNK
    band = (LEFT_CHUNKS + 1) * CHUNK
    pad = LEFT_CHUNKS * CHUNK
    scale = D ** -0.5
    kp = jnp.pad(k, ((0, 0), (pad, 0), (0, 0), (0, 0)))
    vp = jnp.pad(v, ((0, 0), (pad, 0), (0, 0), (0, 0)))
    idx = (jnp.arange(nc) * CHUNK)[:, None] + jnp.arange(band)[None, :]
    k_band = kp[:, idx]
    v_band = vp[:, idx]
    q_c = q.reshape(B, nc, CHUNK, H, D)
    s = jnp.einsum('bnqhd,bnkhd->bhnqk', q_c, k_band).astype(jnp.float32) * scale
    dist = jnp.arange(CHUNK)[:, None] + pad - jnp.arange(band)[None, :]
    rel_idx = jnp.clip(dist, -(CHUNK - 1), REL_CLIP) + (CHUNK - 1)
    bias = rel_bias.astype(jnp.float32)[:, rel_idx]
    s = s + bias[None, :, None]
    valid = (idx - pad) >= 0
    s = jnp.where(valid[None, None, :, None, :], s, NEG_INF)
    p = jax.nn.softmax(s, axis=-1)
    o = jnp.einsum('bhnqk,bnkhd->bnqhd', p.astype(v.dtype), v_band)
    return o.reshape(B, S, H, D)


def memory_cross_attention(h, m, w_q, w_kv, w_o):
    B, S, _ = h.shape
    q = (h @ w_q).reshape(B, S, MEM_HEADS, MEM_HEAD_DIM)
    k, v = jnp.split(m @ w_kv, 2, axis=-1)
    k = k.reshape(B, N_MEM, MEM_HEADS, MEM_HEAD_DIM)
    v = v.reshape(B, N_MEM, MEM_HEADS, MEM_HEAD_DIM)
    s = jnp.einsum('bshd,bmhd->bhsm', q, k).astype(jnp.float32) * (MEM_HEAD_DIM ** -0.5)
    p = jax.nn.softmax(s, axis=-1)
    o = jnp.einsum('bhsm,bmhd->bshd', p.astype(v.dtype), v).reshape(B, S, D_MODEL)
    return o @ w_o


def swiglu(h, w_gate_up, w_down):
    g, u = jnp.split(h @ w_gate_up, 2, axis=-1)
    return (jax.nn.silu(g) * u) @ w_down


def setup_inputs(seed: int = 0) -> dict:
    key = jax.random.key(seed)
    ks = jax.random.split(key, 24)

    def nrm(k, shape, scale):
        return scale * jax.random.normal(k, shape, jnp.float32)

    def gain(k, n):
        return 1.0 + nrm(k, (DEPTH, n), 0.05)

    s_d = D_MODEL ** -0.5
    w_in = jnp.concatenate([
        nrm(ks[0], (DEPTH, D_MODEL, 3 * FOX_WIDTH), s_d),
        nrm(ks[1], (DEPTH, D_MODEL, FOX_HEADS), 0.1 * s_d),
        nrm(ks[2], (DEPTH, D_MODEL, 3 * CHUNK_WIDTH), s_d),
    ], axis=-1)
    b_fgate = jnp.linspace(3.0, 7.0, FOX_HEADS)[None, :] + nrm(ks[3], (DEPTH, FOX_HEADS), 0.3)
    return {
        "x": jax.random.normal(ks[4], (BATCH, SEQ, D_MODEL), jnp.float32),
        "mem": jax.random.normal(ks[5], (BATCH, N_MEM, D_MODEL), jnp.float32),
        "g_mix_pre": gain(ks[6], D_MODEL),
        "w_in": w_in,
        "b_fgate": b_fgate,
        "rel_bias": nrm(ks[7], (DEPTH, CHUNK_HEADS, NUM_REL), 0.5),
        "g_fox_out": gain(ks[8], FOX_WIDTH),
        "g_chunk_out": gain(ks[9], CHUNK_WIDTH),
        "w_out": nrm(ks[10], (DEPTH, D_MODEL, D_MODEL), s_d),
        "g_mix_post": gain(ks[11], D_MODEL),
        "g_mem_pre": gain(ks[12], D_MODEL),
        "g_mem_kv": gain(ks[13], D_MODEL),
        "w_mem_q": nrm(ks[14], (DEPTH, D_MODEL, D_MODEL), s_d),
        "w_mem_kv": nrm(ks[15], (DEPTH, D_MODEL, 2 * D_MODEL), s_d),
        "w_mem_o": nrm(ks[16], (DEPTH, D_MODEL, D_MODEL), s_d),
        "g_mem_post": gain(ks[17], D_MODEL),
        "g_ffn_pre": gain(ks[18], D_MODEL),
        "w_gate_up": nrm(ks[19], (DEPTH, D_MODEL, 2 * D_FF), s_d),
        "w_down": nrm(ks[20], (DEPTH, D_FF, D_MODEL), D_FF ** -0.5),
        "g_ffn_post": gain(ks[21], D_MODEL),
    }


def reference(x, mem, g_mix_pre, w_in, b_fgate, rel_bias, g_fox_out, g_chunk_out, w_out,
              g_mix_post, g_mem_pre, g_mem_kv, w_mem_q, w_mem_kv, w_mem_o, g_mem_post,
              g_ffn_pre, w_gate_up, w_down, g_ffn_post):
    B, S, _ = x.shape
    splits = [FOX_WIDTH, 2 * FOX_WIDTH, 3 * FOX_WIDTH, 3 * FOX_WIDTH + FOX_HEADS,
              3 * FOX_WIDTH + FOX_HEADS + CHUNK_WIDTH, 3 * FOX_WIDTH + FOX_HEADS + 2 * CHUNK_WIDTH]
    for l in range(DEPTH):
        h = rms_norm(x, g_mix_pre[l])
        proj = h @ w_in[l]
        q_f, k_f, v_f, f_logit, q_c, k_c, v_c = jnp.split(proj, splits, axis=-1)
        log_f = jax.nn.log_sigmoid((f_logit + b_fgate[l]).astype(jnp.float32))
        o_f = fox_attention(q_f.reshape(B, S, FOX_HEADS, HEAD_DIM),
                            k_f.reshape(B, S, FOX_HEADS, HEAD_DIM),
                            v_f.reshape(B, S, FOX_HEADS, HEAD_DIM), log_f)
        o_c = chunked_rel_attention(q_c.reshape(B, S, CHUNK_HEADS, HEAD_DIM),
                                    k_c.reshape(B, S, CHUNK_HEADS, HEAD_DIM),
                                    v_c.reshape(B, S, CHUNK_HEADS, HEAD_DIM), rel_bias[l])
        o_f = rms_norm(o_f.reshape(B, S, FOX_WIDTH), g_fox_out[l])
        o_c = rms_norm(o_c.reshape(B, S, CHUNK_WIDTH), g_chunk_out[l])
        mix = jnp.concatenate([o_f, o_c], axis=-1) @ w_out[l]
        x = x + rms_norm(mix, g_mix_post[l])
        h = rms_norm(x, g_mem_pre[l])
        m = rms_norm(mem, g_mem_kv[l])
        ca = memory_cross_attention(h, m, w_mem_q[l], w_mem_kv[l], w_mem_o[l])
        x = x + rms_norm(ca, g_mem_post[l])
        h = rms_norm(x, g_ffn_pre[l])
        x = x + rms_norm(swiglu(h, w_gate_up[l], w_down[l]), g_ffn_post[l])
    return x
```

```python
import functools

import jax
import jax.numpy as jnp
import numpy as np
from jax import lax
from jax.experimental import pallas as pl
from jax.experimental.pallas import tpu as pltpu

F32 = jnp.float32
BF16 = jnp.bfloat16

HEAD_DIM = 64
FOX_HEADS = 8
CHUNK_HEADS = 8
CHUNK = 64
LEFT_CHUNKS = 8
REL_CLIP = 128
MEM_HEADS = 4
EPS = 1e-6
NEG_INF = -1e30

LANES = 128
HEAD_PAD = 128
V_ROWS = 80
ONES_LANE = LANES - 1
N_SPLIT = 3
TOKEN_TILE = 512
VMEM_LIMIT = 56 * 1024 * 1024


def _params(semantics):
    return pltpu.CompilerParams(dimension_semantics=semantics, vmem_limit_bytes=VMEM_LIMIT)


def _const_spec(shape):
    nd = len(shape)
    return pl.BlockSpec(shape, lambda *_: (0,) * nd, pipeline_mode=pl.Buffered(1))


def _rms_rows(x, g):
    ms = jnp.mean(x * x, axis=-1, keepdims=True)
    return x * lax.rsqrt(ms + EPS) * g


def _split3_lanes(v):
    lane = lax.broadcasted_iota(jnp.int32, v.shape, 1)
    t1 = v.astype(BF16)
    r1 = v - t1.astype(F32)
    t2 = r1.astype(BF16)
    r2 = r1 - t2.astype(F32)
    t3 = r2.astype(BF16)
    return jnp.where(lane < FOX_HEADS, t1, jnp.where(lane < 2 * FOX_HEADS, t2, t3))


def _dot(a, b):
    return jnp.dot(a, b, preferred_element_type=F32)


def _dot_nt(a, b):
    return lax.dot_general(a, b, (((1,), (1,)), ((), ())), preferred_element_type=F32)


def _inproj_kernel(x_ref, g_ref, wf_ref, bf_ref, wkf_ref, wqf_ref, wvf_ref, wkc_ref, wqc_ref, wvc_ref,
                   ek_ref, eq_ref, ev_ref,
                   kf_ref, qf_ref, vf_ref, kc_ref, qc_ref, vc_ref, carry_ref):
    tm = x_ref.shape[1]

    @pl.when(pl.program_id(1) == 0)
    def _():
        carry_ref[...] = jnp.zeros_like(carry_ref)

    h = _rms_rows(x_ref[0], g_ref[...]).astype(BF16)

    z = _dot(h, wf_ref[...]) + bf_ref[...]
    lf = jnp.minimum(z, 0.0) - jnp.log1p(jnp.exp(-jnp.abs(z)))
    row = lax.broadcasted_iota(jnp.int32, (tm, tm), 0)
    col = lax.broadcasted_iota(jnp.int32, (tm, tm), 1)
    tri = (col <= row).astype(BF16)
    l1 = lf.astype(BF16)
    r1 = lf - l1.astype(F32)
    l2 = r1.astype(BF16)
    l3 = (r1 - l2.astype(F32)).astype(BF16)
    c = _dot(tri, l1) + _dot(tri, l2) + _dot(tri, l3) + carry_ref[...]
    carry_ref[...] = c[tm - 1:tm, :]

    lane = lax.broadcasted_iota(jnp.int32, c.shape, 1)
    gate = jnp.where(lane == ONES_LANE, jnp.ones_like(c).astype(BF16), _split3_lanes(c))

    kf_ref[0, 0] = (_dot(h, wkf_ref[...]) + _dot(gate, ek_ref[...])).astype(BF16)
    qf_ref[0] = (_dot_nt(wqf_ref[...], h) + _dot_nt(eq_ref[...], gate)).astype(BF16)
    ones_rows = _dot_nt(ev_ref[...], gate)
    vf_ref[0, 0] = (_dot_nt(wvf_ref[...], h) + ones_rows).astype(BF16)
    kc_ref[0, 0] = _dot(h, wkc_ref[...]).astype(BF16)
    qc_ref[0] = _dot_nt(wqc_ref[...], h).astype(BF16)
    vc_ref[0, 0] = (_dot_nt(wvc_ref[...], h) + ones_rows).astype(BF16)


def _inproj(x, g, wp):
    B, S, D = x.shape
    tm = TOKEN_TILE
    nk = S // tm
    hp = FOX_HEADS * HEAD_PAD
    hv = FOX_HEADS * V_ROWS
    tile = lambda b, i: (b, i, 0)
    out_shapes = (
        jax.ShapeDtypeStruct((B, nk, tm, hp), BF16),
        jax.ShapeDtypeStruct((B, hp, S), BF16),
        jax.ShapeDtypeStruct((B, nk, hv, tm), BF16),
        jax.ShapeDtypeStruct((B, nk, tm, hp), BF16),
        jax.ShapeDtypeStruct((B, hp, S), BF16),
        jax.ShapeDtypeStruct((B, nk, hv, tm), BF16),
    )
    k_spec = pl.BlockSpec((1, 1, tm, hp), lambda b, i: (b, i, 0, 0))
    q_spec = pl.BlockSpec((1, hp, tm), lambda b, i: (b, 0, i))
    v_spec = pl.BlockSpec((1, 1, hv, tm), lambda b, i: (b, i, 0, 0))
    weights = [g, wp["wf"], wp["bf"], wp["wkf"], wp["wqf"], wp["wvf"], wp["wkc"], wp["wqc"], wp["wvc"],
               wp["ek"], wp["eq"], wp["ev"]]
    return pl.pallas_call(
        _inproj_kernel,
        grid=(B, nk),
        in_specs=[pl.BlockSpec((1, tm, D), tile)] + [_const_spec(w.shape) for w in weights],
        out_specs=(k_spec, q_spec, v_spec, k_spec, q_spec, v_spec),
        out_shape=out_shapes,
        scratch_shapes=[pltpu.VMEM((1, LANES), F32)],
        compiler_params=_params(("arbitrary", "arbitrary")),
        name="inproj",
    )(x, *weights)


def _fox_kernel(q_ref, k_ref, v_ref, o_ref, acc_ref, m_ref):
    iq = pl.program_id(2)
    tq = q_ref.shape[2]
    tk = k_ref.shape[2]
    qT = q_ref[0]
    m_ref[...] = jnp.full(m_ref.shape, NEG_INF, F32)
    acc_ref[...] = jnp.zeros_like(acc_ref)

    def step(j, causal):
        s = _dot(k_ref[0, j], qT)
        if causal:
            key = lax.broadcasted_iota(jnp.int32, (tk, tq), 0)
            qry = lax.broadcasted_iota(jnp.int32, (tk, tq), 1)
            s = jnp.where(key <= qry, s, NEG_INF)
        m_old = m_ref[...]
        m_new = jnp.maximum(m_old, jnp.max(s, axis=0, keepdims=True))
        p = jnp.exp(s - m_new).astype(BF16)
        acc_ref[...] = jnp.exp(m_old - m_new) * acc_ref[...] + _dot(v_ref[0, j], p)
        m_ref[...] = m_new

    def body(j, carry):
        step(j, causal=False)
        return carry

    lax.fori_loop(0, iq, body, 0)
    step(iq, causal=True)
    acc = acc_ref[...]
    o_ref[0] = (acc[:HEAD_DIM] / acc[HEAD_DIM:HEAD_DIM + 1]).astype(o_ref.dtype)


def _fox(kf, qf, vf):
    B, nk, tk, hp = kf.shape
    S = qf.shape[2]
    H = hp // HEAD_PAD
    tq = tk
    return pl.pallas_call(
        _fox_kernel,
        grid=(B, H, S // tq),
        in_specs=[
            pl.BlockSpec((1, HEAD_PAD, tq), lambda b, h, i: (b, h, i)),
            pl.BlockSpec((1, nk, tk, HEAD_PAD), lambda b, h, i: (b, 0, 0, h)),
            pl.BlockSpec((1, nk, V_ROWS, tk), lambda b, h, i: (b, 0, h, 0)),
        ],
        out_specs=pl.BlockSpec((1, HEAD_DIM, tq), lambda b, h, i: (b, h, i)),
        out_shape=jax.ShapeDtypeStruct((B, H * HEAD_DIM, S), BF16),
        scratch_shapes=[pltpu.VMEM((V_ROWS, tq), F32), pltpu.VMEM((1, tq), F32)],
        compiler_params=_params(("arbitrary", "arbitrary", "arbitrary")),
        name="fox",
    )(qf, kf, vf)


def _chunk_kernel(q_ref, kp_ref, kc_ref, vp_ref, vc_ref, bias_ref, o_ref):
    iq = pl.program_id(2)
    qT = q_ref[0]
    s_cur = _dot(kc_ref[0, 0], qT) + bias_ref[0, 1]
    s_prev = _dot(kp_ref[0, 0], qT) + bias_ref[0, 0]
    s_prev = jnp.where(iq > 0, s_prev, NEG_INF)
    m = jnp.maximum(jnp.max(s_cur, axis=0, keepdims=True), jnp.max(s_prev, axis=0, keepdims=True))
    p_cur = jnp.exp(s_cur - m).astype(BF16)
    p_prev = jnp.exp(s_prev - m).astype(BF16)
    acc = _dot(vc_ref[0, 0], p_cur) + _dot(vp_ref[0, 0], p_prev)
    o_ref[0] = (acc[:HEAD_DIM] / acc[HEAD_DIM:HEAD_DIM + 1]).astype(o_ref.dtype)


def _chunk(kc, qc, vc, bias):
    B, nk, tk, hp = kc.shape
    S = qc.shape[2]
    H = hp // HEAD_PAD
    tq = tk
    prev = lambda i: jnp.maximum(i - 1, 0)
    return pl.pallas_call(
        _chunk_kernel,
        grid=(B, H, S // tq),
        in_specs=[
            pl.BlockSpec((1, HEAD_PAD, tq), lambda b, h, i: (b, h, i)),
            pl.BlockSpec((1, 1, tk, HEAD_PAD), lambda b, h, i: (b, prev(i), 0, h)),
            pl.BlockSpec((1, 1, tk, HEAD_PAD), lambda b, h, i: (b, i, 0, h)),
            pl.BlockSpec((1, 1, V_ROWS, tk), lambda b, h, i: (b, prev(i), h, 0)),
            pl.BlockSpec((1, 1, V_ROWS, tk), lambda b, h, i: (b, i, h, 0)),
            pl.BlockSpec((1, 2, tk, tq), lambda b, h, i: (h, 0, 0, 0)),
        ],
        out_specs=pl.BlockSpec((1, HEAD_DIM, tq), lambda b, h, i: (b, h, i)),
        out_shape=jax.ShapeDtypeStruct((B, H * HEAD_DIM, S), BF16),
        compiler_params=_params(("arbitrary", "arbitrary", "arbitrary")),
        name="chunk",
    )(qc, kc, kc, vc, vc, bias)


def _chunk_bias_table(rel_bias, tq):
    key = np.arange(2 * tq)[:, None] - tq
    qry = np.arange(tq)[None, :]
    dist = qry - key
    rel_idx = np.clip(dist, -(CHUNK - 1), REL_CLIP) + (CHUNK - 1)
    kchunk = np.floor_divide(key, CHUNK)
    qchunk = qry // CHUNK
    valid = (kchunk <= qchunk) & (kchunk >= qchunk - LEFT_CHUNKS)
    table = jnp.where(valid[None], rel_bias.astype(F32)[:, rel_idx], NEG_INF)
    return table.reshape(rel_bias.shape[0], 2, tq, tq)


def _memkv_kernel(mem_ref, g_ref, w_ref, kv_ref):
    m = _rms_rows(mem_ref[0], g_ref[...]).astype(BF16)
    kv_ref[0] = _dot(m, w_ref[...]).astype(BF16)


def _memkv(mem, g, w):
    B, N, D = mem.shape
    return pl.pallas_call(
        _memkv_kernel,
        grid=(B,),
        in_specs=[pl.BlockSpec((1, N, D), lambda b: (b, 0, 0)), _const_spec(g.shape), _const_spec(w.shape)],
        out_specs=pl.BlockSpec((1, N, 2 * D), lambda b: (b, 0, 0)),
        out_shape=jax.ShapeDtypeStruct((B, N, 2 * D), BF16),
        compiler_params=_params(("arbitrary",)),
        name="memkv",
    )(mem, g, w)


def _mix_mem_kernel(x_ref, of_ref, oc_ref, kv_ref, gfo_ref, gco_ref, wout_ref, gmp_ref,
                    gmq_ref, wq_ref, wo_ref, gmo_ref, y_ref):
    d = x_ref.shape[2]
    of = jnp.transpose(of_ref[0].astype(F32))
    oc = jnp.transpose(oc_ref[0].astype(F32))
    cat = jnp.concatenate([_rms_rows(of, gfo_ref[...]), _rms_rows(oc, gco_ref[...])], axis=-1).astype(BF16)
    x1 = x_ref[0] + _rms_rows(_dot(cat, wout_ref[...]), gmp_ref[...])

    h = _rms_rows(x1, gmq_ref[...]).astype(BF16)
    q = _dot(h, wq_ref[...]).astype(BF16)
    kv = kv_ref[0]
    dh = d // MEM_HEADS
    outs = []
    for hh in range(MEM_HEADS):
        sl = slice(hh * dh, (hh + 1) * dh)
        s = _dot_nt(q[:, sl], kv[:, sl])
        p = jnp.exp(s - jnp.max(s, axis=-1, keepdims=True))
        l = jnp.sum(p, axis=-1, keepdims=True)
        o = _dot(p.astype(BF16), kv[:, d + hh * dh:d + (hh + 1) * dh])
        outs.append(o / l)
    ca = jnp.concatenate(outs, axis=-1).astype(BF16)
    y_ref[0] = x1 + _rms_rows(_dot(ca, wo_ref[...]), gmo_ref[...])


def _mix_mem(x, of, oc, kv, wp):
    B, S, D = x.shape
    tm = TOKEN_TILE
    half = of.shape[1]
    weights = [wp["g_fox_out"], wp["g_chunk_out"], wp["w_out"], wp["g_mix_post"],
               wp["g_mem_pre"], wp["w_mem_q"], wp["w_mem_o"], wp["g_mem_post"]]
    return pl.pallas_call(
        _mix_mem_kernel,
        grid=(B, S // tm),
        in_specs=[
            pl.BlockSpec((1, tm, D), lambda b, i: (b, i, 0)),
            pl.BlockSpec((1, half, tm), lambda b, i: (b, 0, i)),
            pl.BlockSpec((1, half, tm), lambda b, i: (b, 0, i)),
            pl.BlockSpec((1,) + kv.shape[1:], lambda b, i: (b, 0, 0)),
        ] + [_const_spec(w.shape) for w in weights],
        out_specs=pl.BlockSpec((1, tm, D), lambda b, i: (b, i, 0)),
        out_shape=jax.ShapeDtypeStruct((B, S, D), F32),
        compiler_params=_params(("arbitrary", "arbitrary")),
        name="mix_mem",
    )(x, of, oc, kv, *weights)


FFN_CHUNK = 1024


def _ffn_kernel(x_ref, gpre_ref, wg_ref, wu_ref, wd_ref, gpost_ref, y_ref, act_ref):
    x = x_ref[0]
    h = _rms_rows(x, gpre_ref[...]).astype(BF16)
    dff = wg_ref.shape[1]
    for c0 in range(0, dff, FFN_CHUNK):
        c1 = min(c0 + FFN_CHUNK, dff)
        g = _dot(h, wg_ref[:, c0:c1])
        u = _dot(h, wu_ref[:, c0:c1])
        act_ref[:, c0:c1] = (jax.nn.silu(g) * u).astype(BF16)
    y_ref[0] = x + _rms_rows(_dot(act_ref[...], wd_ref[...]), gpost_ref[...])


def _ffn(x, wp):
    B, S, D = x.shape
    tm = TOKEN_TILE
    weights = [wp["g_ffn_pre"], wp["w_gate"], wp["w_up"], wp["w_down"], wp["g_ffn_post"]]
    dff = wp["w_gate"].shape[1]
    return pl.pallas_call(
        _ffn_kernel,
        grid=(B, S // tm),
        in_specs=[pl.BlockSpec((1, tm, D), lambda b, i: (b, i, 0))] + [_const_spec(w.shape) for w in weights],
        out_specs=pl.BlockSpec((1, tm, D), lambda b, i: (b, i, 0)),
        out_shape=jax.ShapeDtypeStruct((B, S, D), F32),
        scratch_shapes=[pltpu.VMEM((tm, dff), BF16)],
        compiler_params=_params(("arbitrary", "arbitrary")),
        name="ffn",
    )(x, *weights)


def _pad_heads_cols(w, heads, width):
    d = w.shape[0]
    w = w.reshape(d, heads, HEAD_DIM)
    w = jnp.pad(w, ((0, 0), (0, 0), (0, width - HEAD_DIM)))
    return w.reshape(d, heads * width)


def _placement(entries, rows, cols):
    m = np.zeros((rows, cols), np.float32)
    for r, c, v in entries:
        m[r, c] = v
    return jnp.asarray(m, BF16)


def _prep_layer(l, g_mix_pre, w_in, b_fgate, g_fox_out, g_chunk_out, w_out, g_mix_post, g_mem_pre, g_mem_kv,
                w_mem_q, w_mem_kv, w_mem_o, g_mem_post, g_ffn_pre, w_gate_up, w_down, g_ffn_post):
    d = w_in.shape[1]
    fw = FOX_HEADS * HEAD_DIM
    cw = CHUNK_HEADS * HEAD_DIM
    w = w_in[l]
    o = 0
    wq_f, wk_f, wv_f = w[:, o:o + fw], w[:, o + fw:o + 2 * fw], w[:, o + 2 * fw:o + 3 * fw]
    o += 3 * fw
    w_f = w[:, o:o + FOX_HEADS]
    o += FOX_HEADS
    wq_c, wk_c, wv_c = w[:, o:o + cw], w[:, o + cw:o + 2 * cw], w[:, o + 2 * cw:o + 3 * cw]
    scale = HEAD_DIM ** -0.5

    row = lambda v: v.astype(F32)[None, :]
    wf3 = jnp.pad(jnp.tile(w_f, (1, N_SPLIT)), ((0, 0), (0, LANES - N_SPLIT * FOX_HEADS)))
    bf3 = jnp.pad(jnp.tile(b_fgate[l].astype(F32), N_SPLIT), (0, LANES - N_SPLIT * FOX_HEADS))[None, :]

    ek, eq, ev = [], [], []
    for h in range(FOX_HEADS):
        for i in range(N_SPLIT):
            ek.append((ONES_LANE, h * HEAD_PAD + HEAD_DIM + i, 1.0))
            ek.append((i * FOX_HEADS + h, h * HEAD_PAD + HEAD_DIM + N_SPLIT + i, -1.0))
            eq.append((h * HEAD_PAD + HEAD_DIM + i, i * FOX_HEADS + h, 1.0))
            eq.append((h * HEAD_PAD + HEAD_DIM + N_SPLIT + i, ONES_LANE, 1.0))
        ev.append((h * V_ROWS + HEAD_DIM, ONES_LANE, 1.0))

    dff = w_gate_up.shape[2] // 2
    mem_scale = (w_mem_q.shape[2] // MEM_HEADS) ** -0.5
    return dict(
        g_mix_pre=row(g_mix_pre[l]),
        wf=wf3.astype(BF16), bf=bf3,
        wkf=_pad_heads_cols(wk_f, FOX_HEADS, HEAD_PAD).astype(BF16),
        wqf=_pad_heads_cols(wq_f * scale, FOX_HEADS, HEAD_PAD).T.astype(BF16),
        wvf=_pad_heads_cols(wv_f, FOX_HEADS, V_ROWS).T.astype(BF16),
        wkc=_pad_heads_cols(wk_c, CHUNK_HEADS, HEAD_PAD).astype(BF16),
        wqc=_pad_heads_cols(wq_c * scale, CHUNK_HEADS, HEAD_PAD).T.astype(BF16),
        wvc=_pad_heads_cols(wv_c, CHUNK_HEADS, V_ROWS).T.astype(BF16),
        ek=_placement(ek, LANES, FOX_HEADS * HEAD_PAD),
        eq=_placement(eq, FOX_HEADS * HEAD_PAD, LANES),
        ev=_placement(ev, FOX_HEADS * V_ROWS, LANES),
        g_fox_out=row(g_fox_out[l]), g_chunk_out=row(g_chunk_out[l]),
        w_out=w_out[l].astype(BF16), g_mix_post=row(g_mix_post[l]),
        g_mem_pre=row(g_mem_pre[l]), g_mem_kv=row(g_mem_kv[l]),
        w_mem_q=(w_mem_q[l] * mem_scale).astype(BF16), w_mem_kv=w_mem_kv[l].astype(BF16),
        w_mem_o=w_mem_o[l].astype(BF16), g_mem_post=row(g_mem_post[l]),
        g_ffn_pre=row(g_ffn_pre[l]),
        w_gate=w_gate_up[l][:, :dff].astype(BF16), w_up=w_gate_up[l][:, dff:].astype(BF16),
        w_down=w_down[l].astype(BF16), g_ffn_post=row(g_ffn_post[l]),
    )


def kernel(x, mem, g_mix_pre, w_in, b_fgate, rel_bias, g_fox_out, g_chunk_out, w_out, g_mix_post, g_mem_pre,
           g_mem_kv, w_mem_q, w_mem_kv, w_mem_o, g_mem_post, g_ffn_pre, w_gate_up, w_down, g_ffn_post):
    depth = w_in.shape[0]
    for l in range(depth):
        wp = _prep_layer(l, g_mix_pre, w_in, b_fgate, g_fox_out, g_chunk_out, w_out, g_mix_post, g_mem_pre,
                         g_mem_kv, w_mem_q, w_mem_kv, w_mem_o, g_mem_post, g_ffn_pre, w_gate_up, w_down,
                         g_ffn_post)
        kf, qf, vf, kc, qc, vc = _inproj(x, wp["g_mix_pre"], wp)
        o_f = _fox(kf, qf, vf)
        o_c = _chunk(kc, qc, vc, _chunk_bias_table(rel_bias[l], TOKEN_TILE))
        kv = _memkv(mem, wp["g_mem_kv"], wp["w_mem_kv"])
        x = _mix_mem(x, o_f, o_c, kv, wp)
        x = _ffn(x, wp)
    return x
```

```python
import functools

import jax
import jax.numpy as jnp
import numpy as np
from jax import lax
from jax.experimental import pallas as pl
from jax.experimental.pallas import tpu as pltpu

F32 = jnp.float32
BF16 = jnp.bfloat16

HEAD_DIM = 64
FOX_HEADS = 8
CHUNK_HEADS = 8
CHUNK = 64
LEFT_CHUNKS = 8
REL_CLIP = 128
MEM_HEADS = 4
EPS = 1e-6
NEG_INF = -1e30

LANES = 128
HEAD_PAD = 128
V_ROWS = 80
ONES_LANE = LANES - 1
N_SPLIT = 3
TOKEN_TILE = 512
FOX_BLOCK = 1024
LOG2E = 1.4426950408889634
VMEM_LIMIT = 56 * 1024 * 1024


def _params(semantics):
    return pltpu.CompilerParams(dimension_semantics=semantics, vmem_limit_bytes=VMEM_LIMIT)


def _const_spec(shape):
    nd = len(shape)
    return pl.BlockSpec(shape, lambda *_: (0,) * nd, pipeline_mode=pl.Buffered(1))


def _rms_rows(x, g):
    ms = jnp.mean(x * x, axis=-1, keepdims=True)
    return x * lax.rsqrt(ms + EPS) * g


def _split3_lanes(v):
    lane = lax.broadcasted_iota(jnp.int32, v.shape, 1)
    t1 = v.astype(BF16)
    r1 = v - t1.astype(F32)
    t2 = r1.astype(BF16)
    r2 = r1 - t2.astype(F32)
    t3 = r2.astype(BF16)
    return jnp.where(lane < FOX_HEADS, t1, jnp.where(lane < 2 * FOX_HEADS, t2, t3))


def _dot(a, b):
    return jnp.dot(a, b, preferred_element_type=F32)


def _dot_nt(a, b):
    return lax.dot_general(a, b, (((1,), (1,)), ((), ())), preferred_element_type=F32)


def _inproj_kernel(x_ref, g_ref, wf_ref, bf_ref, wkf_ref, wqf_ref, wvf_ref, wkc_ref, wqc_ref, wvc_ref,
                   ek_ref, eq_ref, ev_ref,
                   kf_ref, qf_ref, vf_ref, kc_ref, qc_ref, vc_ref, carry_ref):
    tm = x_ref.shape[1]

    @pl.when(pl.program_id(1) == 0)
    def _():
        carry_ref[...] = jnp.zeros_like(carry_ref)

    h = _rms_rows(x_ref[0], g_ref[...]).astype(BF16)

    z = _dot(h, wf_ref[...]) + bf_ref[...]
    lf = jnp.minimum(z, 0.0) - jnp.log1p(jnp.exp(-jnp.abs(z)))
    row = lax.broadcasted_iota(jnp.int32, (tm, tm), 0)
    col = lax.broadcasted_iota(jnp.int32, (tm, tm), 1)
    tri = (col <= row).astype(BF16)
    l1 = lf.astype(BF16)
    r1 = lf - l1.astype(F32)
    l2 = r1.astype(BF16)
    l3 = (r1 - l2.astype(F32)).astype(BF16)
    c = _dot(tri, l1) + _dot(tri, l2) + _dot(tri, l3) + carry_ref[...]
    carry_ref[...] = c[tm - 1:tm, :]

    lane = lax.broadcasted_iota(jnp.int32, c.shape, 1)
    gate = jnp.where(lane == ONES_LANE, jnp.ones_like(c).astype(BF16), _split3_lanes(c * LOG2E))

    kf_ref[0, 0] = (_dot(h, wkf_ref[...]) + _dot(gate, ek_ref[...])).astype(BF16)
    qf_ref[0] = (_dot_nt(wqf_ref[...], h) + _dot_nt(eq_ref[...], gate)).astype(BF16)
    ones_rows = _dot_nt(ev_ref[...], gate)
    vf_ref[0, 0] = (_dot_nt(wvf_ref[...], h) + ones_rows).astype(BF16)
    kc_ref[0, 0] = _dot(h, wkc_ref[...]).astype(BF16)
    qc_ref[0] = _dot_nt(wqc_ref[...], h).astype(BF16)
    vc_ref[0, 0] = (_dot_nt(wvc_ref[...], h) + ones_rows).astype(BF16)


def _inproj(x, g, wp):
    B, S, D = x.shape
    tm = TOKEN_TILE
    nk = S // tm
    hp = FOX_HEADS * HEAD_PAD
    hv = FOX_HEADS * V_ROWS
    tile = lambda b, i: (b, i, 0)
    out_shapes = (
        jax.ShapeDtypeStruct((B, nk, tm, hp), BF16),
        jax.ShapeDtypeStruct((B, hp, S), BF16),
        jax.ShapeDtypeStruct((B, nk, hv, tm), BF16),
        jax.ShapeDtypeStruct((B, nk, tm, hp), BF16),
        jax.ShapeDtypeStruct((B, hp, S), BF16),
        jax.ShapeDtypeStruct((B, nk, hv, tm), BF16),
    )
    k_spec = pl.BlockSpec((1, 1, tm, hp), lambda b, i: (b, i, 0, 0))
    q_spec = pl.BlockSpec((1, hp, tm), lambda b, i: (b, 0, i))
    v_spec = pl.BlockSpec((1, 1, hv, tm), lambda b, i: (b, i, 0, 0))
    weights = [g, wp["wf"], wp["bf"], wp["wkf"], wp["wqf"], wp["wvf"], wp["wkc"], wp["wqc"], wp["wvc"],
               wp["ek"], wp["eq"], wp["ev"]]
    return pl.pallas_call(
        _inproj_kernel,
        grid=(B, nk),
        in_specs=[pl.BlockSpec((1, tm, D), tile)] + [_const_spec(w.shape) for w in weights],
        out_specs=(k_spec, q_spec, v_spec, k_spec, q_spec, v_spec),
        out_shape=out_shapes,
        scratch_shapes=[pltpu.VMEM((1, LANES), F32)],
        compiler_params=_params(("arbitrary", "arbitrary")),
        name="inproj",
    )(x, *weights)


def _fox_kernel(q_ref, k_ref, v_ref, o_ref, s0_ref, s1_ref, mx0_ref, mx1_ref, acc_ref, m_ref):
    iq = pl.program_id(2)
    tq = q_ref.shape[2]
    tk = k_ref.shape[2]
    tv = v_ref.shape[3]
    qT = q_ref[0]
    m_ref[...] = jnp.full(m_ref.shape, NEG_INF, F32)
    acc_ref[...] = jnp.zeros_like(acc_ref)

    def scores_into(j, s_ref, mx_ref):
        s = _dot(k_ref[0, j], qT)
        s_ref[...] = s
        mx_ref[...] = jnp.max(s, axis=0, keepdims=True)

    def accumulate(j, s, mx):
        m_old = m_ref[...]
        m_new = jnp.maximum(m_old, mx)
        p = jnp.exp2(s - m_new).astype(BF16)
        pv = _dot(v_ref[0, j * (tk // tv)], p[:tv])
        for r in range(1, tk // tv):
            pv = pv + _dot(v_ref[0, j * (tk // tv) + r], p[r * tv:(r + 1) * tv])
        acc_ref[...] = jnp.exp2(m_old - m_new) * acc_ref[...] + pv
        m_ref[...] = m_new

    def diagonal(s_ref):
        key = lax.broadcasted_iota(jnp.int32, (tk, tq), 0)
        qry = lax.broadcasted_iota(jnp.int32, (tk, tq), 1)
        s = jnp.where(key <= qry, s_ref[...], NEG_INF)
        accumulate(iq, s, jnp.max(s, axis=0, keepdims=True))

    scores_into(0, s0_ref, mx0_ref)

    def pair(t, carry):
        j = 2 * t
        scores_into(j + 1, s1_ref, mx1_ref)
        accumulate(j, s0_ref[...], mx0_ref[...])
        scores_into(j + 2, s0_ref, mx0_ref)
        accumulate(j + 1, s1_ref[...], mx1_ref[...])
        return carry

    lax.fori_loop(0, lax.shift_right_logical(iq, 1), pair, 0)
    odd = lax.bitwise_and(iq, 1)

    @pl.when(odd == 0)
    def _():
        diagonal(s0_ref)

    @pl.when(odd == 1)
    def _():
        scores_into(iq, s1_ref, mx1_ref)
        accumulate(iq - 1, s0_ref[...], mx0_ref[...])
        diagonal(s1_ref)

    acc = acc_ref[...]
    o_ref[0] = (acc[:HEAD_DIM] / acc[HEAD_DIM:HEAD_DIM + 1]).astype(o_ref.dtype)


def _fox(kf, qf, vf):
    B, nv, tv, hp = kf.shape
    S = qf.shape[2]
    H = hp // HEAD_PAD
    tq = tk = FOX_BLOCK
    kf = kf.reshape(B, S // tk, tk, hp)
    return pl.pallas_call(
        _fox_kernel,
        grid=(B, H, S // tq),
        in_specs=[
            pl.BlockSpec((1, HEAD_PAD, tq), lambda b, h, i: (b, h, i)),
            pl.BlockSpec((1, S // tk, tk, HEAD_PAD), lambda b, h, i: (b, 0, 0, h)),
            pl.BlockSpec((1, nv, V_ROWS, tv), lambda b, h, i: (b, 0, h, 0)),
        ],
        out_specs=pl.BlockSpec((1, HEAD_DIM, tq), lambda b, h, i: (b, h, i)),
        out_shape=jax.ShapeDtypeStruct((B, H * HEAD_DIM, S), BF16),
        scratch_shapes=[pltpu.VMEM((tk, tq), F32), pltpu.VMEM((tk, tq), F32),
                        pltpu.VMEM((1, tq), F32), pltpu.VMEM((1, tq), F32),
                        pltpu.VMEM((V_ROWS, tq), F32), pltpu.VMEM((1, tq), F32)],
        compiler_params=_params(("arbitrary", "arbitrary", "arbitrary")),
        name="fox",
    )(qf, kf, vf)


def _chunk_kernel(q_ref, kp_ref, kc_ref, vp_ref, vc_ref, bias_ref, o_ref):
    iq = pl.program_id(2)
    qT = q_ref[0]
    s_cur = _dot(kc_ref[0, 0], qT) + bias_ref[0, 1]
    s_prev = _dot(kp_ref[0, 0], qT) + bias_ref[0, 0]
    s_prev = jnp.where(iq > 0, s_prev, NEG_INF)
    m = jnp.maximum(jnp.max(s_cur, axis=0, keepdims=True), jnp.max(s_prev, axis=0, keepdims=True))
    p_cur = jnp.exp2(s_cur - m).astype(BF16)
    p_prev = jnp.exp2(s_prev - m).astype(BF16)
    acc = _dot(vc_ref[0, 0], p_cur) + _dot(vp_ref[0, 0], p_prev)
    o_ref[0] = (acc[:HEAD_DIM] / acc[HEAD_DIM:HEAD_DIM + 1]).astype(o_ref.dtype)


def _chunk(kc, qc, vc, bias):
    B, nk, tk, hp = kc.shape
    S = qc.shape[2]
    H = hp // HEAD_PAD
    tq = tk
    prev = lambda i: jnp.maximum(i - 1, 0)
    return pl.pallas_call(
        _chunk_kernel,
        grid=(B, H, S // tq),
        in_specs=[
            pl.BlockSpec((1, HEAD_PAD, tq), lambda b, h, i: (b, h, i)),
            pl.BlockSpec((1, 1, tk, HEAD_PAD), lambda b, h, i: (b, prev(i), 0, h)),
            pl.BlockSpec((1, 1, tk, HEAD_PAD), lambda b, h, i: (b, i, 0, h)),
            pl.BlockSpec((1, 1, V_ROWS, tk), lambda b, h, i: (b, prev(i), h, 0)),
            pl.BlockSpec((1, 1, V_ROWS, tk), lambda b, h, i: (b, i, h, 0)),
            pl.BlockSpec((1, 2, tk, tq), lambda b, h, i: (h, 0, 0, 0)),
        ],
        out_specs=pl.BlockSpec((1, HEAD_DIM, tq), lambda b, h, i: (b, h, i)),
        out_shape=jax.ShapeDtypeStruct((B, H * HEAD_DIM, S), BF16),
        compiler_params=_params(("arbitrary", "arbitrary", "arbitrary")),
        name="chunk",
    )(qc, kc, kc, vc, vc, bias)


def _chunk_bias_table(rel_bias, tq):
    heads = rel_bias.shape[0]
    rows = 2 * tq
    span = 3 * tq - 1
    dist = np.arange(span) - (tq - 1)
    by_dist = rel_bias.astype(F32)[:, np.clip(dist, -(CHUNK - 1), REL_CLIP) + (CHUNK - 1)]
    tiled = jnp.tile(jnp.pad(by_dist, ((0, 0), (0, 1))), (1, rows))[:, :rows * span]
    toeplitz = tiled.reshape(heads, rows, span)[:, :, rows - 1:rows - 1 + tq]
    key = np.arange(rows)[:, None] - tq
    qry = np.arange(tq)[None, :]
    kchunk = np.floor_divide(key, CHUNK)
    qchunk = qry // CHUNK
    valid = (kchunk <= qchunk) & (kchunk >= qchunk - LEFT_CHUNKS)
    table = jnp.where(valid[None], toeplitz * LOG2E, NEG_INF)
    return table.reshape(heads, 2, tq, tq)


def _memkv_kernel(mem_ref, g_ref, w_ref, kv_ref):
    m = _rms_rows(mem_ref[0], g_ref[...]).astype(BF16)
    kv_ref[0] = _dot(m, w_ref[...]).astype(BF16)


def _memkv(mem, g, w):
    B, N, D = mem.shape
    return pl.pallas_call(
        _memkv_kernel,
        grid=(B,),
        in_specs=[pl.BlockSpec((1, N, D), lambda b: (b, 0, 0)), _const_spec(g.shape), _const_spec(w.shape)],
        out_specs=pl.BlockSpec((1, N, 2 * D), lambda b: (b, 0, 0)),
        out_shape=jax.ShapeDtypeStruct((B, N, 2 * D), BF16),
        compiler_params=_params(("arbitrary",)),
        name="memkv",
    )(mem, g, w)


def _mix_mem_kernel(x_ref, of_ref, oc_ref, kv_ref, gfo_ref, gco_ref, wout_ref, gmp_ref,
                    gmq_ref, wq_ref, wo_ref, gmo_ref, y_ref):
    d = x_ref.shape[2]
    of = jnp.transpose(of_ref[0].astype(F32))
    oc = jnp.transpose(oc_ref[0].astype(F32))
    cat = jnp.concatenate([_rms_rows(of, gfo_ref[...]), _rms_rows(oc, gco_ref[...])], axis=-1).astype(BF16)
    x1 = x_ref[0] + _rms_rows(_dot(cat, wout_ref[...]), gmp_ref[...])

    h = _rms_rows(x1, gmq_ref[...]).astype(BF16)
    q = _dot(h, wq_ref[...]).astype(BF16)
    kv = kv_ref[0]
    dh = d // MEM_HEADS
    outs = []
    for hh in range(MEM_HEADS):
        sl = slice(hh * dh, (hh + 1) * dh)
        s = _dot_nt(q[:, sl], kv[:, sl])
        p = jnp.exp2(s - jnp.max(s, axis=-1, keepdims=True))
        l = jnp.sum(p, axis=-1, keepdims=True)
        o = _dot(p.astype(BF16), kv[:, d + hh * dh:d + (hh + 1) * dh])
        outs.append(o / l)
    ca = jnp.concatenate(outs, axis=-1).astype(BF16)
    y_ref[0] = x1 + _rms_rows(_dot(ca, wo_ref[...]), gmo_ref[...])


def _mix_mem(x, of, oc, kv, wp):
    B, S, D = x.shape
    tm = TOKEN_TILE
    half = of.shape[1]
    weights = [wp["g_fox_out"], wp["g_chunk_out"], wp["w_out"], wp["g_mix_post"],
               wp["g_mem_pre"], wp["w_mem_q"], wp["w_mem_o"], wp["g_mem_post"]]
    return pl.pallas_call(
        _mix_mem_kernel,
        grid=(B, S // tm),
        in_specs=[
            pl.BlockSpec((1, tm, D), lambda b, i: (b, i, 0)),
            pl.BlockSpec((1, half, tm), lambda b, i: (b, 0, i)),
            pl.BlockSpec((1, half, tm), lambda b, i: (b, 0, i)),
            pl.BlockSpec((1,) + kv.shape[1:], lambda b, i: (b, 0, 0)),
        ] + [_const_spec(w.shape) for w in weights],
        out_specs=pl.BlockSpec((1, tm, D), lambda b, i: (b, i, 0)),
        out_shape=jax.ShapeDtypeStruct((B, S, D), F32),
        compiler_params=_params(("arbitrary", "arbitrary")),
        name="mix_mem",
    )(x, of, oc, kv, *weights)


FFN_CHUNK = 1024


def _ffn_kernel(x_ref, gpre_ref, wg_ref, wu_ref, wd_ref, gpost_ref, y_ref, act_ref):
    x = x_ref[0]
    h = _rms_rows(x, gpre_ref[...]).astype(BF16)
    dff = wg_ref.shape[1]
    for c0 in range(0, dff, FFN_CHUNK):
        c1 = min(c0 + FFN_CHUNK, dff)
        g = _dot(h, wg_ref[:, c0:c1])
        u = _dot(h, wu_ref[:, c0:c1])
        act_ref[:, c0:c1] = (jax.nn.silu(g) * u).astype(BF16)
    y_ref[0] = x + _rms_rows(_dot(act_ref[...], wd_ref[...]), gpost_ref[...])


def _ffn(x, wp):
    B, S, D = x.shape
    tm = TOKEN_TILE
    weights = [wp["g_ffn_pre"], wp["w_gate"], wp["w_up"], wp["w_down"], wp["g_ffn_post"]]
    dff = wp["w_gate"].shape[1]
    return pl.pallas_call(
        _ffn_kernel,
        grid=(B, S // tm),
        in_specs=[pl.BlockSpec((1, tm, D), lambda b, i: (b, i, 0))] + [_const_spec(w.shape) for w in weights],
        out_specs=pl.BlockSpec((1, tm, D), lambda b, i: (b, i, 0)),
        out_shape=jax.ShapeDtypeStruct((B, S, D), F32),
        scratch_shapes=[pltpu.VMEM((tm, dff), BF16)],
        compiler_params=_params(("arbitrary", "arbitrary")),
        name="ffn",
    )(x, *weights)


def _pad_heads_cols(w, heads, width):
    d = w.shape[0]
    w = w.reshape(d, heads, HEAD_DIM)
    w = jnp.pad(w, ((0, 0), (0, 0), (0, width - HEAD_DIM)))
    return w.reshape(d, heads * width)


def _placement(entries, rows, cols):
    m = np.zeros((rows, cols), np.float32)
    for r, c, v in entries:
        m[r, c] = v
    return jnp.asarray(m, BF16)


def _prep_layer(l, g_mix_pre, w_in, b_fgate, g_fox_out, g_chunk_out, w_out, g_mix_post, g_mem_pre, g_mem_kv,
                w_mem_q, w_mem_kv, w_mem_o, g_mem_post, g_ffn_pre, w_gate_up, w_down, g_ffn_post):
    d = w_in.shape[1]
    fw = FOX_HEADS * HEAD_DIM
    cw = CHUNK_HEADS * HEAD_DIM
    w = w_in[l]
    o = 0
    wq_f, wk_f, wv_f = w[:, o:o + fw], w[:, o + fw:o + 2 * fw], w[:, o + 2 * fw:o + 3 * fw]
    o += 3 * fw
    w_f = w[:, o:o + FOX_HEADS]
    o += FOX_HEADS
    wq_c, wk_c, wv_c = w[:, o:o + cw], w[:, o + cw:o + 2 * cw], w[:, o + 2 * cw:o + 3 * cw]
    scale = HEAD_DIM ** -0.5 * LOG2E

    row = lambda v: v.astype(F32)[None, :]
    wf3 = jnp.pad(jnp.tile(w_f, (1, N_SPLIT)), ((0, 0), (0, LANES - N_SPLIT * FOX_HEADS)))
    bf3 = jnp.pad(jnp.tile(b_fgate[l].astype(F32), N_SPLIT), (0, LANES - N_SPLIT * FOX_HEADS))[None, :]

    ek, eq, ev = [], [], []
    for h in range(FOX_HEADS):
        for i in range(N_SPLIT):
            ek.append((ONES_LANE, h * HEAD_PAD + HEAD_DIM + i, 1.0))
            ek.append((i * FOX_HEADS + h, h * HEAD_PAD + HEAD_DIM + N_SPLIT + i, -1.0))
            eq.append((h * HEAD_PAD + HEAD_DIM + i, i * FOX_HEADS + h, 1.0))
            eq.append((h * HEAD_PAD + HEAD_DIM + N_SPLIT + i, ONES_LANE, 1.0))
        ev.append((h * V_ROWS + HEAD_DIM, ONES_LANE, 1.0))

    dff = w_gate_up.shape[2] // 2
    mem_scale = (w_mem_q.shape[2] // MEM_HEADS) ** -0.5 * LOG2E
    return dict(
        g_mix_pre=row(g_mix_pre[l]),
        wf=wf3.astype(BF16), bf=bf3,
        wkf=_pad_heads_cols(wk_f, FOX_HEADS, HEAD_PAD).astype(BF16),
        wqf=_pad_heads_cols(wq_f * scale, FOX_HEADS, HEAD_PAD).T.astype(BF16),
        wvf=_pad_heads_cols(wv_f, FOX_HEADS, V_ROWS).T.astype(BF16),
        wkc=_pad_heads_cols(wk_c, CHUNK_HEADS, HEAD_PAD).astype(BF16),
        wqc=_pad_heads_cols(wq_c * scale, CHUNK_HEADS, HEAD_PAD).T.astype(BF16),
        wvc=_pad_heads_cols(wv_c, CHUNK_HEADS, V_ROWS).T.astype(BF16),
        ek=_placement(ek, LANES, FOX_HEADS * HEAD_PAD),
        eq=_placement(eq, FOX_HEADS * HEAD_PAD, LANES),
        ev=_placement(ev, FOX_HEADS * V_ROWS, LANES),
        g_fox_out=row(g_fox_out[l]), g_chunk_out=row(g_chunk_out[l]),
        w_out=w_out[l].astype(BF16), g_mix_post=row(g_mix_post[l]),
        g_mem_pre=row(g_mem_pre[l]), g_mem_kv=row(g_mem_kv[l]),
        w_mem_q=(w_mem_q[l] * mem_scale).astype(BF16), w_mem_kv=w_mem_kv[l].astype(BF16),
        w_mem_o=w_mem_o[l].astype(BF16), g_mem_post=row(g_mem_post[l]),
        g_ffn_pre=row(g_ffn_pre[l]),
        w_gate=w_gate_up[l][:, :dff].astype(BF16), w_up=w_gate_up[l][:, dff:].astype(BF16),
        w_down=w_down[l].astype(BF16), g_ffn_post=row(g_ffn_post[l]),
    )


def kernel(x, mem, g_mix_pre, w_in, b_fgate, rel_bias, g_fox_out, g_chunk_out, w_out, g_mix_post, g_mem_pre,
           g_mem_kv, w_mem_q, w_mem_kv, w_mem_o, g_mem_post, g_ffn_pre, w_gate_up, w_down, g_ffn_post):
    depth = w_in.shape[0]
    for l in range(depth):
        wp = _prep_layer(l, g_mix_pre, w_in, b_fgate, g_fox_out, g_chunk_out, w_out, g_mix_post, g_mem_pre,
                         g_mem_kv, w_mem_q, w_mem_kv, w_mem_o, g_mem_post, g_ffn_pre, w_gate_up, w_down,
                         g_ffn_post)
        kf, qf, vf, kc, qc, vc = _inproj(x, wp["g_mix_pre"], wp)
        o_f = _fox(kf, qf, vf)
        o_c = _chunk(kc, qc, vc, _chunk_bias_table(rel_bias[l], TOKEN_TILE))
        kv = _memkv(mem, wp["g_mem_kv"], wp["w_mem_kv"])
        x = _mix_mem(x, o_f, o_c, kv, wp)
        x = _ffn(x, wp)
    return x
```

```python
import functools

import jax
import jax.numpy as jnp
import numpy as np
from jax import lax
from jax.experimental import pallas as pl
from jax.experimental.pallas import tpu as pltpu

F32 = jnp.float32
BF16 = jnp.bfloat16

HEAD_DIM = 64
FOX_HEADS = 8
CHUNK_HEADS = 8
CHUNK = 64
LEFT_CHUNKS = 8
REL_CLIP = 128
MEM_HEADS = 4
EPS = 1e-6
NEG_INF = -1e30

LANES = 128
HEAD_PAD = 128
V_ROWS = 80
ONES_LANE = LANES - 1
N_SPLIT = 3
TOKEN_TILE = 512
FOX_BLOCK = 1024
LOG2E = 1.4426950408889634
VMEM_LIMIT = 56 * 1024 * 1024


def _params(semantics):
    return pltpu.CompilerParams(dimension_semantics=semantics, vmem_limit_bytes=VMEM_LIMIT)


def _const_spec(shape):
    nd = len(shape)
    return pl.BlockSpec(shape, lambda *_: (0,) * nd, pipeline_mode=pl.Buffered(1))


def _rms_rows(x, g):
    ms = jnp.mean(x * x, axis=-1, keepdims=True)
    return x * lax.rsqrt(ms + EPS) * g


def _split3_lanes(v):
    lane = lax.broadcasted_iota(jnp.int32, v.shape, 1)
    t1 = v.astype(BF16)
    r1 = v - t1.astype(F32)
    t2 = r1.astype(BF16)
    r2 = r1 - t2.astype(F32)
    t3 = r2.astype(BF16)
    return jnp.where(lane < FOX_HEADS, t1, jnp.where(lane < 2 * FOX_HEADS, t2, t3))


def _dot(a, b):
    return jnp.dot(a, b, preferred_element_type=F32)


def _dot_nt(a, b):
    return lax.dot_general(a, b, (((1,), (1,)), ((), ())), preferred_element_type=F32)


def _inproj_kernel(x_ref, g_ref, wf_ref, bf_ref, wkf_ref, wqf_ref, wvf_ref, wkc_ref, wqc_ref, wvc_ref,
                   ek_ref, eq_ref, ev_ref,
                   kf_ref, qf_ref, vf_ref, kc_ref, qc_ref, vc_ref, carry_ref):
    tm = x_ref.shape[1]

    @pl.when(pl.program_id(1) == 0)
    def _():
        carry_ref[...] = jnp.zeros_like(carry_ref)

    h = _rms_rows(x_ref[0], g_ref[...]).astype(BF16)

    z = _dot(h, wf_ref[...]) + bf_ref[...]
    lf = jnp.minimum(z, 0.0) - jnp.log1p(jnp.exp(-jnp.abs(z)))
    row = lax.broadcasted_iota(jnp.int32, (tm, tm), 0)
    col = lax.broadcasted_iota(jnp.int32, (tm, tm), 1)
    tri = (col <= row).astype(BF16)
    l1 = lf.astype(BF16)
    r1 = lf - l1.astype(F32)
    l2 = r1.astype(BF16)
    l3 = (r1 - l2.astype(F32)).astype(BF16)
    c = _dot(tri, l1) + _dot(tri, l2) + _dot(tri, l3) + carry_ref[...]
    carry_ref[...] = c[tm - 1:tm, :]

    lane = lax.broadcasted_iota(jnp.int32, c.shape, 1)
    gate = jnp.where(lane == ONES_LANE, jnp.ones_like(c).astype(BF16), _split3_lanes(c * LOG2E))

    kf_ref[0, 0] = (_dot(h, wkf_ref[...]) + _dot(gate, ek_ref[...])).astype(BF16)
    qf_ref[0] = (_dot_nt(wqf_ref[...], h) + _dot_nt(eq_ref[...], gate)).astype(BF16)
    ones_rows = _dot_nt(ev_ref[...], gate)
    vf_ref[0, 0] = (_dot_nt(wvf_ref[...], h) + ones_rows).astype(BF16)
    kc_ref[0, 0] = _dot(h, wkc_ref[...]).astype(BF16)
    qc_ref[0] = _dot_nt(wqc_ref[...], h).astype(BF16)
    vc_ref[0, 0] = (_dot_nt(wvc_ref[...], h) + ones_rows).astype(BF16)


def _inproj(x, g, wp):
    B, S, D = x.shape
    tm = TOKEN_TILE
    nk = S // tm
    hp = FOX_HEADS * HEAD_PAD
    hc = CHUNK_HEADS * HEAD_DIM
    hv = FOX_HEADS * V_ROWS
    tile = lambda b, i: (b, i, 0)
    out_shapes = (
        jax.ShapeDtypeStruct((B, nk, tm, hp), BF16),
        jax.ShapeDtypeStruct((B, hp, S), BF16),
        jax.ShapeDtypeStruct((B, nk, hv, tm), BF16),
        jax.ShapeDtypeStruct((B, nk, tm, hc), BF16),
        jax.ShapeDtypeStruct((B, hc, S), BF16),
        jax.ShapeDtypeStruct((B, nk, hv, tm), BF16),
    )
    k_spec = lambda w: pl.BlockSpec((1, 1, tm, w), lambda b, i: (b, i, 0, 0))
    q_spec = lambda w: pl.BlockSpec((1, w, tm), lambda b, i: (b, 0, i))
    v_spec = pl.BlockSpec((1, 1, hv, tm), lambda b, i: (b, i, 0, 0))
    weights = [g, wp["wf"], wp["bf"], wp["wkf"], wp["wqf"], wp["wvf"], wp["wkc"], wp["wqc"], wp["wvc"],
               wp["ek"], wp["eq"], wp["ev"]]
    return pl.pallas_call(
        _inproj_kernel,
        grid=(B, nk),
        in_specs=[pl.BlockSpec((1, tm, D), tile)] + [_const_spec(w.shape) for w in weights],
        out_specs=(k_spec(hp), q_spec(hp), v_spec, k_spec(hc), q_spec(hc), v_spec),
        out_shape=out_shapes,
        scratch_shapes=[pltpu.VMEM((1, LANES), F32)],
        compiler_params=_params(("arbitrary", "arbitrary")),
        name="inproj",
    )(x, *weights)


def _fox_kernel(q_ref, k_ref, v_ref, o_ref, s0_ref, s1_ref, mx0_ref, mx1_ref, acc_ref, m_ref):
    iq = pl.program_id(2)
    tq = q_ref.shape[2]
    tk = k_ref.shape[2]
    tv = v_ref.shape[3]
    qT = q_ref[0]
    m_ref[...] = jnp.full(m_ref.shape, NEG_INF, F32)
    acc_ref[...] = jnp.zeros_like(acc_ref)

    def scores_into(j, s_ref, mx_ref):
        s = _dot(k_ref[0, j], qT)
        s_ref[...] = s
        mx_ref[...] = jnp.max(s, axis=0, keepdims=True)

    def accumulate(j, s, mx):
        m_old = m_ref[...]
        m_new = jnp.maximum(m_old, mx)
        p = jnp.exp2(s - m_new).astype(BF16)
        pv = _dot(v_ref[0, j * (tk // tv)], p[:tv])
        for r in range(1, tk // tv):
            pv = pv + _dot(v_ref[0, j * (tk // tv) + r], p[r * tv:(r + 1) * tv])
        acc_ref[...] = jnp.exp2(m_old - m_new) * acc_ref[...] + pv
        m_ref[...] = m_new

    def diagonal(s_ref):
        key = lax.broadcasted_iota(jnp.int32, (tk, tq), 0)
        qry = lax.broadcasted_iota(jnp.int32, (tk, tq), 1)
        s = jnp.where(key <= qry, s_ref[...], NEG_INF)
        accumulate(iq, s, jnp.max(s, axis=0, keepdims=True))

    scores_into(0, s0_ref, mx0_ref)

    def pair(t, carry):
        j = 2 * t
        scores_into(j + 1, s1_ref, mx1_ref)
        accumulate(j, s0_ref[...], mx0_ref[...])
        scores_into(j + 2, s0_ref, mx0_ref)
        accumulate(j + 1, s1_ref[...], mx1_ref[...])
        return carry

    lax.fori_loop(0, lax.shift_right_logical(iq, 1), pair, 0)
    odd = lax.bitwise_and(iq, 1)

    @pl.when(odd == 0)
    def _():
        diagonal(s0_ref)

    @pl.when(odd == 1)
    def _():
        scores_into(iq, s1_ref, mx1_ref)
        accumulate(iq - 1, s0_ref[...], mx0_ref[...])
        diagonal(s1_ref)

    acc = acc_ref[...]
    o_ref[0] = (acc[:HEAD_DIM] / acc[HEAD_DIM:HEAD_DIM + 1]).astype(o_ref.dtype)


def _fox(kf, qf, vf):
    B, nv, tv, hp = kf.shape
    S = qf.shape[2]
    H = hp // HEAD_PAD
    tq = tk = FOX_BLOCK
    kf = kf.reshape(B, S // tk, tk, hp)
    return pl.pallas_call(
        _fox_kernel,
        grid=(B, H, S // tq),
        in_specs=[
            pl.BlockSpec((1, HEAD_PAD, tq), lambda b, h, i: (b, h, i)),
            pl.BlockSpec((1, S // tk, tk, HEAD_PAD), lambda b, h, i: (b, 0, 0, h)),
            pl.BlockSpec((1, nv, V_ROWS, tv), lambda b, h, i: (b, 0, h, 0)),
        ],
        out_specs=pl.BlockSpec((1, HEAD_DIM, tq), lambda b, h, i: (b, h, i)),
        out_shape=jax.ShapeDtypeStruct((B, H * HEAD_DIM, S), BF16),
        scratch_shapes=[pltpu.VMEM((tk, tq), F32), pltpu.VMEM((tk, tq), F32),
                        pltpu.VMEM((1, tq), F32), pltpu.VMEM((1, tq), F32),
                        pltpu.VMEM((V_ROWS, tq), F32), pltpu.VMEM((1, tq), F32)],
        compiler_params=_params(("arbitrary", "arbitrary", "arbitrary")),
        name="fox",
    )(qf, kf, vf)


BAND_BLOCKS = LEFT_CHUNKS * CHUNK // LANES


def _bias_block_id(delta):
    if delta < 0 or delta > BAND_BLOCKS:
        return None
    return {0: 0, 1: 1, BAND_BLOCKS: 3}.get(delta, 2)


def _chunk_kernel(q_ref, kp_ref, kc_ref, vp_ref, vc_ref, blk_ref, o_ref):
    iq = pl.program_id(1)
    tq = q_ref.shape[2]
    nb = tq // LANES
    before_start = jnp.where(iq > 0, 0.0, NEG_INF).astype(F32)
    zero_blk = jnp.zeros((LANES, LANES), BF16)
    row = lax.broadcasted_iota(jnp.int32, (2 * HEAD_DIM, tq), 0)
    for h in range(CHUNK_HEADS):
        pair, odd = divmod(h, 2)
        lanes = slice(pair * 2 * HEAD_DIM, (pair + 1) * 2 * HEAD_DIM)
        q_pair = q_ref[0, lanes, :]
        qT = jnp.where((row >= HEAD_DIM) == bool(odd), q_pair, jnp.zeros_like(q_pair))
        s_half = (_dot(kp_ref[0, 0, :, lanes], qT), _dot(kc_ref[0, 0, :, lanes], qT))
        p_blocks = {}
        for b in range(nb):
            cols = slice(b * LANES, (b + 1) * LANES)
            entries = []
            for half in (0, 1):
                for al in range(nb):
                    bid = _bias_block_id(b - (al - nb * (1 - half)))
                    if bid is None:
                        continue
                    sb = s_half[half][al * LANES:(al + 1) * LANES, cols] + blk_ref[h, bid]
                    if half == 0:
                        sb = sb + before_start
                    entries.append((half, al, sb))
            m = functools.reduce(jnp.maximum, [jnp.max(sb, axis=0, keepdims=True) for _, _, sb in entries])
            for half, al, sb in entries:
                p_blocks[half, al, b] = jnp.exp2(sb - m).astype(BF16)
        p_half = [jnp.concatenate([jnp.concatenate([p_blocks.get((half, al, b), zero_blk) for b in range(nb)],
                                                   axis=1) for al in range(nb)], axis=0) for half in (0, 1)]
        vrows = slice(h * V_ROWS, (h + 1) * V_ROWS)
        acc = _dot(vc_ref[0, 0, vrows, :], p_half[1]) + _dot(vp_ref[0, 0, vrows, :], p_half[0])
        o_ref[0, h * HEAD_DIM:(h + 1) * HEAD_DIM, :] = (
            acc[:HEAD_DIM] / acc[HEAD_DIM:HEAD_DIM + 1]).astype(o_ref.dtype)


def _chunk(kc, qc, vc, blocks):
    B, nk, tk, hc = kc.shape
    S = qc.shape[2]
    hv = vc.shape[2]
    tq = tk
    assert tq == LEFT_CHUNKS * CHUNK and 2 * CHUNK == LANES
    prev = lambda i: jnp.maximum(i - 1, 0)
    return pl.pallas_call(
        _chunk_kernel,
        grid=(B, S // tq),
        in_specs=[
            pl.BlockSpec((1, hc, tq), lambda b, i: (b, 0, i)),
            pl.BlockSpec((1, 1, tk, hc), lambda b, i: (b, prev(i), 0, 0)),
            pl.BlockSpec((1, 1, tk, hc), lambda b, i: (b, i, 0, 0)),
            pl.BlockSpec((1, 1, hv, tk), lambda b, i: (b, prev(i), 0, 0)),
            pl.BlockSpec((1, 1, hv, tk), lambda b, i: (b, i, 0, 0)),
            _const_spec(blocks.shape),
        ],
        out_specs=pl.BlockSpec((1, hc, tq), lambda b, i: (b, 0, i)),
        out_shape=jax.ShapeDtypeStruct((B, hc, S), BF16),
        compiler_params=_params(("arbitrary", "arbitrary")),
        name="chunk",
    )(qc, kc, kc, vc, vc, blocks)


def _chunk_bias_blocks(rel_bias):
    heads = rel_bias.shape[0]
    n = LANES
    span = 3 * n - 1
    dist = np.arange(span) - (n - 1)
    by_dist = rel_bias.astype(F32)[:, np.clip(dist, -(CHUNK - 1), REL_CLIP) + (CHUNK - 1)] * LOG2E
    padded = jnp.pad(by_dist, ((0, 0), (0, 1)))
    skew = jnp.broadcast_to(padded[:, None, :], (heads, n, span + 1)).reshape(heads, n * (span + 1))
    near = skew[:, :n * span].reshape(heads, n, span)[:, :, n - 1:]
    key_chunk = (np.arange(n) // CHUNK)[:, None]
    qry_chunk = (np.arange(n) // CHUNK)[None, :]
    clipped = jnp.broadcast_to(by_dist[:, -1][:, None, None], (heads, n, n))
    blocks = [
        jnp.where((qry_chunk >= key_chunk)[None], near[:, :, :n], NEG_INF),
        near[:, :, n:],
        clipped,
        jnp.where((2 * BAND_BLOCKS + qry_chunk - key_chunk <= LEFT_CHUNKS)[None], clipped, NEG_INF),
    ]
    return jnp.stack(blocks, axis=1)


def _memkv_kernel(mem_ref, g_ref, w_ref, kv_ref):
    m = _rms_rows(mem_ref[0], g_ref[...]).astype(BF16)
    kv_ref[0] = _dot(m, w_ref[...]).astype(BF16)


def _memkv(mem, g, w):
    B, N, D = mem.shape
    return pl.pallas_call(
        _memkv_kernel,
        grid=(B,),
        in_specs=[pl.BlockSpec((1, N, D), lambda b: (b, 0, 0)), _const_spec(g.shape), _const_spec(w.shape)],
        out_specs=pl.BlockSpec((1, N, 2 * D), lambda b: (b, 0, 0)),
        out_shape=jax.ShapeDtypeStruct((B, N, 2 * D), BF16),
        compiler_params=_params(("arbitrary",)),
        name="memkv",
    )(mem, g, w)


def _mix_mem_kernel(x_ref, of_ref, oc_ref, kv_ref, gfo_ref, gco_ref, wout_ref, gmp_ref,
                    gmq_ref, wq_ref, wo_ref, gmo_ref, y_ref):
    d = x_ref.shape[2]
    of = jnp.transpose(of_ref[0].astype(F32))
    oc = jnp.transpose(oc_ref[0].astype(F32))
    cat = jnp.concatenate([_rms_rows(of, gfo_ref[...]), _rms_rows(oc, gco_ref[...])], axis=-1).astype(BF16)
    x1 = x_ref[0] + _rms_rows(_dot(cat, wout_ref[...]), gmp_ref[...])

    h = _rms_rows(x1, gmq_ref[...]).astype(BF16)
    q = _dot(h, wq_ref[...]).astype(BF16)
    kv = kv_ref[0]
    dh = d // MEM_HEADS
    outs = []
    for hh in range(MEM_HEADS):
        sl = slice(hh * dh, (hh + 1) * dh)
        s = _dot_nt(q[:, sl], kv[:, sl])
        p = jnp.exp2(s - jnp.max(s, axis=-1, keepdims=True))
        l = jnp.sum(p, axis=-1, keepdims=True)
        o = _dot(p.astype(BF16), kv[:, d + hh * dh:d + (hh + 1) * dh])
        outs.append(o / l)
    ca = jnp.concatenate(outs, axis=-1).astype(BF16)
    y_ref[0] = x1 + _rms_rows(_dot(ca, wo_ref[...]), gmo_ref[...])


def _mix_mem(x, of, oc, kv, wp):
    B, S, D = x.shape
    tm = TOKEN_TILE
    half = of.shape[1]
    weights = [wp["g_fox_out"], wp["g_chunk_out"], wp["w_out"], wp["g_mix_post"],
               wp["g_mem_pre"], wp["w_mem_q"], wp["w_mem_o"], wp["g_mem_post"]]
    return pl.pallas_call(
        _mix_mem_kernel,
        grid=(B, S // tm),
        in_specs=[
            pl.BlockSpec((1, tm, D), lambda b, i: (b, i, 0)),
            pl.BlockSpec((1, half, tm), lambda b, i: (b, 0, i)),
            pl.BlockSpec((1, half, tm), lambda b, i: (b, 0, i)),
            pl.BlockSpec((1,) + kv.shape[1:], lambda b, i: (b, 0, 0)),
        ] + [_const_spec(w.shape) for w in weights],
        out_specs=pl.BlockSpec((1, tm, D), lambda b, i: (b, i, 0)),
        out_shape=jax.ShapeDtypeStruct((B, S, D), F32),
        compiler_params=_params(("arbitrary", "arbitrary")),
        name="mix_mem",
    )(x, of, oc, kv, *weights)


FFN_CHUNK = 1024


def _ffn_kernel(x_ref, gpre_ref, wg_ref, wu_ref, wd_ref, gpost_ref, y_ref, act_ref):
    x = x_ref[0]
    h = _rms_rows(x, gpre_ref[...]).astype(BF16)
    dff = wg_ref.shape[1]
    for c0 in range(0, dff, FFN_CHUNK):
        c1 = min(c0 + FFN_CHUNK, dff)
        g = _dot(h, wg_ref[:, c0:c1])
        u = _dot(h, wu_ref[:, c0:c1])
        act_ref[:, c0:c1] = (jax.nn.silu(g) * u).astype(BF16)
    y_ref[0] = x + _rms_rows(_dot(act_ref[...], wd_ref[...]), gpost_ref[...])


def _ffn(x, wp):
    B, S, D = x.shape
    tm = TOKEN_TILE
    weights = [wp["g_ffn_pre"], wp["w_gate"], wp["w_up"], wp["w_down"], wp["g_ffn_post"]]
    dff = wp["w_gate"].shape[1]
    return pl.pallas_call(
        _ffn_kernel,
        grid=(B, S // tm),
        in_specs=[pl.BlockSpec((1, tm, D), lambda b, i: (b, i, 0))] + [_const_spec(w.shape) for w in weights],
        out_specs=pl.BlockSpec((1, tm, D), lambda b, i: (b, i, 0)),
        out_shape=jax.ShapeDtypeStruct((B, S, D), F32),
        scratch_shapes=[pltpu.VMEM((tm, dff), BF16)],
        compiler_params=_params(("arbitrary", "arbitrary")),
        name="ffn",
    )(x, *weights)


def _pad_heads_cols(w, heads, width):
    d = w.shape[0]
    w = w.reshape(d, heads, HEAD_DIM)
    w = jnp.pad(w, ((0, 0), (0, 0), (0, width - HEAD_DIM)))
    return w.reshape(d, heads * width)


def _placement(entries, rows, cols):
    m = np.zeros((rows, cols), np.float32)
    for r, c, v in entries:
        m[r, c] = v
    return jnp.asarray(m, BF16)


def _prep_layer(l, g_mix_pre, w_in, b_fgate, g_fox_out, g_chunk_out, w_out, g_mix_post, g_mem_pre, g_mem_kv,
                w_mem_q, w_mem_kv, w_mem_o, g_mem_post, g_ffn_pre, w_gate_up, w_down, g_ffn_post):
    d = w_in.shape[1]
    fw = FOX_HEADS * HEAD_DIM
    cw = CHUNK_HEADS * HEAD_DIM
    w = w_in[l]
    o = 0
    wq_f, wk_f, wv_f = w[:, o:o + fw], w[:, o + fw:o + 2 * fw], w[:, o + 2 * fw:o + 3 * fw]
    o += 3 * fw
    w_f = w[:, o:o + FOX_HEADS]
    o += FOX_HEADS
    wq_c, wk_c, wv_c = w[:, o:o + cw], w[:, o + cw:o + 2 * cw], w[:, o + 2 * cw:o + 3 * cw]
    scale = HEAD_DIM ** -0.5 * LOG2E

    row = lambda v: v.astype(F32)[None, :]
    wf3 = jnp.pad(jnp.tile(w_f, (1, N_SPLIT)), ((0, 0), (0, LANES - N_SPLIT * FOX_HEADS)))
    bf3 = jnp.pad(jnp.tile(b_fgate[l].astype(F32), N_SPLIT), (0, LANES - N_SPLIT * FOX_HEADS))[None, :]

    ek, eq, ev = [], [], []
    for h in range(FOX_HEADS):
        for i in range(N_SPLIT):
            ek.append((ONES_LANE, h * HEAD_PAD + HEAD_DIM + i, 1.0))
            ek.append((i * FOX_HEADS + h, h * HEAD_PAD + HEAD_DIM + N_SPLIT + i, -1.0))
            eq.append((h * HEAD_PAD + HEAD_DIM + i, i * FOX_HEADS + h, 1.0))
            eq.append((h * HEAD_PAD + HEAD_DIM + N_SPLIT + i, ONES_LANE, 1.0))
        ev.append((h * V_ROWS + HEAD_DIM, ONES_LANE, 1.0))

    dff = w_gate_up.shape[2] // 2
    mem_scale = (w_mem_q.shape[2] // MEM_HEADS) ** -0.5 * LOG2E
    return dict(
        g_mix_pre=row(g_mix_pre[l]),
        wf=wf3.astype(BF16), bf=bf3,
        wkf=_pad_heads_cols(wk_f, FOX_HEADS, HEAD_PAD).astype(BF16),
        wqf=_pad_heads_cols(wq_f * scale, FOX_HEADS, HEAD_PAD).T.astype(BF16),
        wvf=_pad_heads_cols(wv_f, FOX_HEADS, V_ROWS).T.astype(BF16),
        wkc=wk_c.astype(BF16),
        wqc=(wq_c * scale).T.astype(BF16),
        wvc=_pad_heads_cols(wv_c, CHUNK_HEADS, V_ROWS).T.astype(BF16),
        ek=_placement(ek, LANES, FOX_HEADS * HEAD_PAD),
        eq=_placement(eq, FOX_HEADS * HEAD_PAD, LANES),
        ev=_placement(ev, FOX_HEADS * V_ROWS, LANES),
        g_fox_out=row(g_fox_out[l]), g_chunk_out=row(g_chunk_out[l]),
        w_out=w_out[l].astype(BF16), g_mix_post=row(g_mix_post[l]),
        g_mem_pre=row(g_mem_pre[l]), g_mem_kv=row(g_mem_kv[l]),
        w_mem_q=(w_mem_q[l] * mem_scale).astype(BF16), w_mem_kv=w_mem_kv[l].astype(BF16),
        w_mem_o=w_mem_o[l].astype(BF16), g_mem_post=row(g_mem_post[l]),
        g_ffn_pre=row(g_ffn_pre[l]),
        w_gate=w_gate_up[l][:, :dff].astype(BF16), w_up=w_gate_up[l][:, dff:].astype(BF16),
        w_down=w_down[l].astype(BF16), g_ffn_post=row(g_ffn_post[l]),
    )


def kernel(x, mem, g_mix_pre, w_in, b_fgate, rel_bias, g_fox_out, g_chunk_out, w_out, g_mix_post, g_mem_pre,
           g_mem_kv, w_mem_q, w_mem_kv, w_mem_o, g_mem_post, g_ffn_pre, w_gate_up, w_down, g_ffn_post):
    depth = w_in.shape[0]
    for l in range(depth):
        wp = _prep_layer(l, g_mix_pre, w_in, b_fgate, g_fox_out, g_chunk_out, w_out, g_mix_post, g_mem_pre,
                         g_mem_kv, w_mem_q, w_mem_kv, w_mem_o, g_mem_post, g_ffn_pre, w_gate_up, w_down,
                         g_ffn_post)
        kf, qf, vf, kc, qc, vc = _inproj(x, wp["g_mix_pre"], wp)
        o_f = _fox(kf, qf, vf)
        o_c = _chunk(kc, qc, vc, _chunk_bias_blocks(rel_bias[l]))
        kv = _memkv(mem, wp["g_mem_kv"], wp["w_mem_kv"])
        x = _mix_mem(x, o_f, o_c, kv, wp)
        x = _ffn(x, wp)
    return x
```

```python
import functools

import jax
import jax.numpy as jnp
import numpy as np
from jax import lax
from jax.experimental import pallas as pl
from jax.experimental.pallas import tpu as pltpu

F32 = jnp.float32
BF16 = jnp.bfloat16

HEAD_DIM = 64
FOX_HEADS = 8
CHUNK_HEADS = 8
CHUNK = 64
LEFT_CHUNKS = 8
REL_CLIP = 128
MEM_HEADS = 4
EPS = 1e-6
NEG_INF = -1e30

LANES = 128
HEAD_PAD = 128
V_ROWS = 80
ONES_LANE = LANES - 1
N_SPLIT = 3
TOKEN_TILE = 512
FOX_BLOCK = 1024
LOG2E = 1.4426950408889634
SKIP_LOG2 = 150.0
NORM_MARGIN = 1.02
VMEM_LIMIT = 56 * 1024 * 1024


def _params(semantics):
    return pltpu.CompilerParams(dimension_semantics=semantics, vmem_limit_bytes=VMEM_LIMIT)


def _const_spec(shape):
    nd = len(shape)
    return pl.BlockSpec(shape, lambda *_: (0,) * nd, pipeline_mode=pl.Buffered(1))


def _rms_rows(x, g):
    ms = jnp.mean(x * x, axis=-1, keepdims=True)
    return x * lax.rsqrt(ms + EPS) * g


def _split3_lanes(v):
    lane = lax.broadcasted_iota(jnp.int32, v.shape, 1)
    t1 = v.astype(BF16)
    r1 = v - t1.astype(F32)
    t2 = r1.astype(BF16)
    r2 = r1 - t2.astype(F32)
    t3 = r2.astype(BF16)
    return jnp.where(lane < FOX_HEADS, t1, jnp.where(lane < 2 * FOX_HEADS, t2, t3))


def _dot(a, b):
    return jnp.dot(a, b, preferred_element_type=F32)


def _dot_nt(a, b):
    return lax.dot_general(a, b, (((1,), (1,)), ((), ())), preferred_element_type=F32)


def _inproj_kernel(x_ref, g_ref, wf_ref, bf_ref, wkf_ref, wqf_ref, wvf_ref, wkc_ref, wqc_ref, wvc_ref,
                   ek_ref, eq_ref, ev_ref, gk_ref, gq_ref,
                   kf_ref, qf_ref, vf_ref, kc_ref, qc_ref, vc_ref, ks_ref, qs_ref, carry_ref):
    tm = x_ref.shape[1]

    @pl.when(pl.program_id(1) == 0)
    def _():
        carry_ref[...] = jnp.zeros_like(carry_ref)

    h = _rms_rows(x_ref[0], g_ref[...]).astype(BF16)

    z = _dot(h, wf_ref[...]) + bf_ref[...]
    lf = jnp.minimum(z, 0.0) - jnp.log1p(jnp.exp(-jnp.abs(z)))
    row = lax.broadcasted_iota(jnp.int32, (tm, tm), 0)
    col = lax.broadcasted_iota(jnp.int32, (tm, tm), 1)
    tri = (col <= row).astype(BF16)
    l1 = lf.astype(BF16)
    r1 = lf - l1.astype(F32)
    l2 = r1.astype(BF16)
    l3 = (r1 - l2.astype(F32)).astype(BF16)
    c = _dot(tri, l1) + _dot(tri, l2) + _dot(tri, l3) + carry_ref[...]
    carry_ref[...] = c[tm - 1:tm, :]

    lane = lax.broadcasted_iota(jnp.int32, c.shape, 1)
    gate = jnp.where(lane == ONES_LANE, jnp.ones_like(c).astype(BF16), _split3_lanes(c * LOG2E))

    kf = (_dot(h, wkf_ref[...]) + _dot(gate, ek_ref[...])).astype(BF16)
    qf = (_dot_nt(wqf_ref[...], h) + _dot_nt(eq_ref[...], gate)).astype(BF16)
    kf_ref[0, 0] = kf
    qf_ref[0] = qf

    kf32 = kf.astype(F32)
    qf32 = qf.astype(F32)
    k_norm2 = jnp.max(_dot((kf32 * kf32).astype(BF16), gk_ref[...]), axis=0, keepdims=True)
    q_norm2 = jnp.max(_dot(gq_ref[...], (qf32 * qf32).astype(BF16)), axis=1, keepdims=True)
    c2 = c * LOG2E
    srow = lax.broadcasted_iota(jnp.int32, ks_ref.shape[2:], 0)
    ks_ref[0, 0] = jnp.where(srow == 0, c2[tm - 1:tm, :], jnp.where(srow == 1, c2[0:1, :], k_norm2))
    qs_ref[0, 0] = jnp.broadcast_to(q_norm2, qs_ref.shape[2:])
    ones_rows = _dot_nt(ev_ref[...], gate)
    vf_ref[0, 0] = (_dot_nt(wvf_ref[...], h) + ones_rows).astype(BF16)
    kc_ref[0, 0] = _dot(h, wkc_ref[...]).astype(BF16)
    qc_ref[0] = _dot_nt(wqc_ref[...], h).astype(BF16)
    vc_ref[0, 0] = (_dot_nt(wvc_ref[...], h) + ones_rows).astype(BF16)


def _inproj(x, g, wp):
    B, S, D = x.shape
    tm = TOKEN_TILE
    nk = S // tm
    hp = FOX_HEADS * HEAD_PAD
    hc = CHUNK_HEADS * HEAD_DIM
    hv = FOX_HEADS * V_ROWS
    tile = lambda b, i: (b, i, 0)
    out_shapes = (
        jax.ShapeDtypeStruct((B, nk, tm, hp), BF16),
        jax.ShapeDtypeStruct((B, hp, S), BF16),
        jax.ShapeDtypeStruct((B, nk, hv, tm), BF16),
        jax.ShapeDtypeStruct((B, nk, tm, hc), BF16),
        jax.ShapeDtypeStruct((B, hc, S), BF16),
        jax.ShapeDtypeStruct((B, nk, hv, tm), BF16),
        jax.ShapeDtypeStruct((B, nk, 8, LANES), F32),
        jax.ShapeDtypeStruct((B, nk, 2 * FOX_HEADS, LANES), F32),
    )
    stat_spec = lambda r: pl.BlockSpec((1, 1, r, LANES), lambda b, i: (b, i, 0, 0))
    k_spec = lambda w: pl.BlockSpec((1, 1, tm, w), lambda b, i: (b, i, 0, 0))
    q_spec = lambda w: pl.BlockSpec((1, w, tm), lambda b, i: (b, 0, i))
    v_spec = pl.BlockSpec((1, 1, hv, tm), lambda b, i: (b, i, 0, 0))
    weights = [g, wp["wf"], wp["bf"], wp["wkf"], wp["wqf"], wp["wvf"], wp["wkc"], wp["wqc"], wp["wvc"],
               wp["ek"], wp["eq"], wp["ev"], wp["gk"], wp["gq"]]
    return pl.pallas_call(
        _inproj_kernel,
        grid=(B, nk),
        in_specs=[pl.BlockSpec((1, tm, D), tile)] + [_const_spec(w.shape) for w in weights],
        out_specs=(k_spec(hp), q_spec(hp), v_spec, k_spec(hc), q_spec(hc), v_spec,
                   stat_spec(8), stat_spec(2 * FOX_HEADS)),
        out_shape=out_shapes,
        scratch_shapes=[pltpu.VMEM((1, LANES), F32)],
        compiler_params=_params(("arbitrary", "arbitrary")),
        name="inproj",
    )(x, *weights)


def _fox_kernel(thr_ref, ck_ref, q_ref, qn_ref, k_ref, v_ref, o_ref, s0_ref, s1_ref, mx0_ref, mx1_ref,
                sf_ref, mxf_ref, acc_ref, m_ref):
    iq = pl.program_id(2)
    nq = pl.num_programs(2)
    tq = q_ref.shape[2]
    nkb, tk = k_ref.shape[1:3]
    tv = v_ref.shape[3]
    qT = q_ref[0]
    m_ref[...] = jnp.full(m_ref.shape, NEG_INF, F32)
    acc_ref[...] = jnp.zeros_like(acc_ref)
    cur = lax.bitwise_and(iq, 1)
    nxt = 1 - cur
    head = pl.program_id(0) * pl.num_programs(1) + pl.program_id(1)

    def first_block(i):
        thr = thr_ref[head * nq + i]
        count = jnp.int32(0)
        for j in range(nkb):
            skip = jnp.logical_and(j < i, thr < ck_ref[head * nkb + j])
            count = count + jnp.where(skip, 1, 0)
        return count

    j0 = first_block(iq)
    j0_next = first_block(jnp.minimum(iq + 1, nq - 1))

    def scores_into(j, s_ref, mx_ref, q=None):
        s = _dot(k_ref[0, j], qT if q is None else q)
        s_ref[...] = s
        mx_ref[...] = jnp.max(s, axis=0, keepdims=True)

    def accumulate(j, s, mx):
        m_old = m_ref[...]
        m_new = jnp.maximum(m_old, mx)
        p = jnp.exp2(s - m_new).astype(BF16)
        pv = _dot(v_ref[0, j * (tk // tv)], p[:tv])
        for r in range(1, tk // tv):
            pv = pv + _dot(v_ref[0, j * (tk // tv) + r], p[r * tv:(r + 1) * tv])
        acc_ref[...] = jnp.exp2(m_old - m_new) * acc_ref[...] + pv
        m_ref[...] = m_new

    def diagonal_and_handover(s_ref):
        scores_into(j0_next, sf_ref.at[nxt], mxf_ref.at[nxt], q=qn_ref[0])
        key = lax.broadcasted_iota(jnp.int32, (tk, tq), 0)
        qry = lax.broadcasted_iota(jnp.int32, (tk, tq), 1)
        s = jnp.where(key <= qry, s_ref[...], NEG_INF)
        accumulate(iq, s, jnp.max(s, axis=0, keepdims=True))

    @pl.when(iq == 0)
    def _():
        scores_into(0, sf_ref.at[cur], mxf_ref.at[cur])

    @pl.when(j0 == iq)
    def _():
        diagonal_and_handover(sf_ref.at[cur])

    @pl.when(j0 < iq)
    def _():
        scores_into(j0 + 1, s1_ref, mx1_ref)
        accumulate(j0, sf_ref[cur], mxf_ref[cur])
        rest = iq - j0 - 1

        def pair(t, carry):
            j = j0 + 2 * t + 1
            scores_into(j + 1, s0_ref, mx0_ref)
            accumulate(j, s1_ref[...], mx1_ref[...])
            scores_into(j + 2, s1_ref, mx1_ref)
            accumulate(j + 1, s0_ref[...], mx0_ref[...])
            return carry

        lax.fori_loop(0, lax.shift_right_logical(rest, 1), pair, 0)
        odd = lax.bitwise_and(rest, 1)

        @pl.when(odd == 0)
        def _():
            diagonal_and_handover(s1_ref)

        @pl.when(odd == 1)
        def _():
            scores_into(iq, s0_ref, mx0_ref)
            accumulate(iq - 1, s1_ref[...], mx1_ref[...])
            diagonal_and_handover(s0_ref)

    acc = acc_ref[...]
    o_ref[0] = (acc[:HEAD_DIM] / acc[HEAD_DIM:HEAD_DIM + 1]).astype(o_ref.dtype)


def _fox_skip_tables(kstats, qstats):
    r = FOX_BLOCK // TOKEN_TILE
    B, nt = kstats.shape[:2]
    H = FOX_HEADS
    per_block = lambda a: a.reshape(B, nt // r, r, H)
    ck = per_block(kstats[:, :, 0, :H])[:, :, r - 1]
    cq = per_block(kstats[:, :, 1, :H])[:, :, 0]
    k_norm = jnp.sqrt(jnp.max(kstats[:, :, 2, :H], axis=1))
    q_norm = jnp.sqrt(jnp.max(per_block(qstats[:, :, :H, 0]), axis=2))
    thr = cq + 2.0 * NORM_MARGIN * q_norm * k_norm[:, None, :] + SKIP_LOG2
    flat = lambda a: jnp.transpose(a, (0, 2, 1)).reshape(-1).astype(F32)
    return flat(thr), flat(ck)


def _fox(kf, qf, vf, kstats, qstats):
    B, nv, tv, hp = kf.shape
    S = qf.shape[2]
    H = hp // HEAD_PAD
    tq = tk = FOX_BLOCK
    kf = kf.reshape(B, S // tk, tk, hp)
    nq = S // tq
    thr, ck = _fox_skip_tables(kstats, qstats)
    grid_spec = pltpu.PrefetchScalarGridSpec(
        num_scalar_prefetch=2,
        grid=(B, H, nq),
        in_specs=[
            pl.BlockSpec((1, HEAD_PAD, tq), lambda b, h, i, *_: (b, h, i)),
            pl.BlockSpec((1, HEAD_PAD, tq), lambda b, h, i, *_: (b, h, jnp.minimum(i + 1, nq - 1))),
            pl.BlockSpec((1, S // tk, tk, HEAD_PAD), lambda b, h, i, *_: (b, 0, 0, h)),
            pl.BlockSpec((1, nv, V_ROWS, tv), lambda b, h, i, *_: (b, 0, h, 0)),
        ],
        out_specs=pl.BlockSpec((1, HEAD_DIM, tq), lambda b, h, i, *_: (b, h, i)),
        scratch_shapes=[pltpu.VMEM((tk, tq), F32), pltpu.VMEM((tk, tq), F32),
                        pltpu.VMEM((1, tq), F32), pltpu.VMEM((1, tq), F32),
                        pltpu.VMEM((2, tk, tq), F32), pltpu.VMEM((2, 1, tq), F32),
                        pltpu.VMEM((V_ROWS, tq), F32), pltpu.VMEM((1, tq), F32)],
    )
    return pl.pallas_call(
        _fox_kernel,
        grid_spec=grid_spec,
        out_shape=jax.ShapeDtypeStruct((B, H * HEAD_DIM, S), BF16),
        compiler_params=_params(("arbitrary", "arbitrary", "arbitrary")),
        name="fox",
    )(thr, ck, qf, qf, kf, vf)


BAND_BLOCKS = LEFT_CHUNKS * CHUNK // LANES


def _bias_block_id(delta):
    if delta < 0 or delta > BAND_BLOCKS:
        return None
    return {0: 0, 1: 1, BAND_BLOCKS: 3}.get(delta, 2)


def _chunk_kernel(q_ref, kp_ref, kc_ref, vp_ref, vc_ref, blk_ref, o_ref):
    iq = pl.program_id(1)
    tq = q_ref.shape[2]
    nb = tq // LANES
    before_start = jnp.where(iq > 0, 0.0, NEG_INF).astype(F32)
    zero_blk = jnp.zeros((LANES, LANES), BF16)
    row = lax.broadcasted_iota(jnp.int32, (2 * HEAD_DIM, tq), 0)
    for h in range(CHUNK_HEADS):
        pair, odd = divmod(h, 2)
        lanes = slice(pair * 2 * HEAD_DIM, (pair + 1) * 2 * HEAD_DIM)
        q_pair = q_ref[0, lanes, :]
        qT = jnp.where((row >= HEAD_DIM) == bool(odd), q_pair, jnp.zeros_like(q_pair))
        s_half = (_dot(kp_ref[0, 0, :, lanes], qT), _dot(kc_ref[0, 0, :, lanes], qT))
        p_blocks = {}
        for b in range(nb):
            cols = slice(b * LANES, (b + 1) * LANES)
            entries = []
            for half in (0, 1):
                for al in range(nb):
                    bid = _bias_block_id(b - (al - nb * (1 - half)))
                    if bid is None:
                        continue
                    sb = s_half[half][al * LANES:(al + 1) * LANES, cols] + blk_ref[h, bid]
                    if half == 0:
                        sb = sb + before_start
                    entries.append((half, al, sb))
            m = functools.reduce(jnp.maximum, [jnp.max(sb, axis=0, keepdims=True) for _, _, sb in entries])
            for half, al, sb in entries:
                p_blocks[half, al, b] = jnp.exp2(sb - m).astype(BF16)
        p_half = [jnp.concatenate([jnp.concatenate([p_blocks.get((half, al, b), zero_blk) for b in range(nb)],
                                                   axis=1) for al in range(nb)], axis=0) for half in (0, 1)]
        vrows = slice(h * V_ROWS, (h + 1) * V_ROWS)
        acc = _dot(vc_ref[0, 0, vrows, :], p_half[1]) + _dot(vp_ref[0, 0, vrows, :], p_half[0])
        o_ref[0, h * HEAD_DIM:(h + 1) * HEAD_DIM, :] = (
            acc[:HEAD_DIM] / acc[HEAD_DIM:HEAD_DIM + 1]).astype(o_ref.dtype)


def _chunk(kc, qc, vc, blocks):
    B, nk, tk, hc = kc.shape
    S = qc.shape[2]
    hv = vc.shape[2]
    tq = tk
    assert tq == LEFT_CHUNKS * CHUNK and 2 * CHUNK == LANES
    prev = lambda i: jnp.maximum(i - 1, 0)
    return pl.pallas_call(
        _chunk_kernel,
        grid=(B, S // tq),
        in_specs=[
            pl.BlockSpec((1, hc, tq), lambda b, i: (b, 0, i)),
            pl.BlockSpec((1, 1, tk, hc), lambda b, i: (b, prev(i), 0, 0)),
            pl.BlockSpec((1, 1, tk, hc), lambda b, i: (b, i, 0, 0)),
            pl.BlockSpec((1, 1, hv, tk), lambda b, i: (b, prev(i), 0, 0)),
            pl.BlockSpec((1, 1, hv, tk), lambda b, i: (b, i, 0, 0)),
            _const_spec(blocks.shape),
        ],
        out_specs=pl.BlockSpec((1, hc, tq), lambda b, i: (b, 0, i)),
        out_shape=jax.ShapeDtypeStruct((B, hc, S), BF16),
        compiler_params=_params(("arbitrary", "arbitrary")),
        name="chunk",
    )(qc, kc, kc, vc, vc, blocks)


def _chunk_bias_blocks(rel_bias):
    heads = rel_bias.shape[0]
    n = LANES
    span = 3 * n - 1
    dist = np.arange(span) - (n - 1)
    by_dist = rel_bias.astype(F32)[:, np.clip(dist, -(CHUNK - 1), REL_CLIP) + (CHUNK - 1)] * LOG2E
    padded = jnp.pad(by_dist, ((0, 0), (0, 1)))
    skew = jnp.broadcast_to(padded[:, None, :], (heads, n, span + 1)).reshape(heads, n * (span + 1))
    near = skew[:, :n * span].reshape(heads, n, span)[:, :, n - 1:]
    key_chunk = (np.arange(n) // CHUNK)[:, None]
    qry_chunk = (np.arange(n) // CHUNK)[None, :]
    clipped = jnp.broadcast_to(by_dist[:, -1][:, None, None], (heads, n, n))
    blocks = [
        jnp.where((qry_chunk >= key_chunk)[None], near[:, :, :n], NEG_INF),
        near[:, :, n:],
        clipped,
        jnp.where((2 * BAND_BLOCKS + qry_chunk - key_chunk <= LEFT_CHUNKS)[None], clipped, NEG_INF),
    ]
    return jnp.stack(blocks, axis=1)


def _memkv_kernel(mem_ref, g_ref, w_ref, kv_ref):
    m = _rms_rows(mem_ref[0], g_ref[...]).astype(BF16)
    kv_ref[0] = _dot(m, w_ref[...]).astype(BF16)


def _memkv(mem, g, w):
    B, N, D = mem.shape
    return pl.pallas_call(
        _memkv_kernel,
        grid=(B,),
        in_specs=[pl.BlockSpec((1, N, D), lambda b: (b, 0, 0)), _const_spec(g.shape), _const_spec(w.shape)],
        out_specs=pl.BlockSpec((1, N, 2 * D), lambda b: (b, 0, 0)),
        out_shape=jax.ShapeDtypeStruct((B, N, 2 * D), BF16),
        compiler_params=_params(("arbitrary",)),
        name="memkv",
    )(mem, g, w)


def _mix_mem_kernel(x_ref, of_ref, oc_ref, kv_ref, gfo_ref, gco_ref, wout_ref, gmp_ref,
                    gmq_ref, wq_ref, wo_ref, gmo_ref, y_ref):
    d = x_ref.shape[2]
    of = jnp.transpose(of_ref[0].astype(F32))
    oc = jnp.transpose(oc_ref[0].astype(F32))
    cat = jnp.concatenate([_rms_rows(of, gfo_ref[...]), _rms_rows(oc, gco_ref[...])], axis=-1).astype(BF16)
    x1 = x_ref[0] + _rms_rows(_dot(cat, wout_ref[...]), gmp_ref[...])

    h = _rms_rows(x1, gmq_ref[...]).astype(BF16)
    q = _dot(h, wq_ref[...]).astype(BF16)
    kv = kv_ref[0]
    dh = d // MEM_HEADS
    outs = []
    for hh in range(MEM_HEADS):
        sl = slice(hh * dh, (hh + 1) * dh)
        s = _dot_nt(q[:, sl], kv[:, sl])
        p = jnp.exp2(s - jnp.max(s, axis=-1, keepdims=True))
        l = jnp.sum(p, axis=-1, keepdims=True)
        o = _dot(p.astype(BF16), kv[:, d + hh * dh:d + (hh + 1) * dh])
        outs.append(o / l)
    ca = jnp.concatenate(outs, axis=-1).astype(BF16)
    y_ref[0] = x1 + _rms_rows(_dot(ca, wo_ref[...]), gmo_ref[...])


def _mix_mem(x, of, oc, kv, wp):
    B, S, D = x.shape
    tm = TOKEN_TILE
    half = of.shape[1]
    weights = [wp["g_fox_out"], wp["g_chunk_out"], wp["w_out"], wp["g_mix_post"],
               wp["g_mem_pre"], wp["w_mem_q"], wp["w_mem_o"], wp["g_mem_post"]]
    return pl.pallas_call(
        _mix_mem_kernel,
        grid=(B, S // tm),
        in_specs=[
            pl.BlockSpec((1, tm, D), lambda b, i: (b, i, 0)),
            pl.BlockSpec((1, half, tm), lambda b, i: (b, 0, i)),
            pl.BlockSpec((1, half, tm), lambda b, i: (b, 0, i)),
            pl.BlockSpec((1,) + kv.shape[1:], lambda b, i: (b, 0, 0)),
        ] + [_const_spec(w.shape) for w in weights],
        out_specs=pl.BlockSpec((1, tm, D), lambda b, i: (b, i, 0)),
        out_shape=jax.ShapeDtypeStruct((B, S, D), F32),
        compiler_params=_params(("arbitrary", "arbitrary")),
        name="mix_mem",
    )(x, of, oc, kv, *weights)


FFN_CHUNK = 1024


def _ffn_kernel(x_ref, gpre_ref, wg_ref, wu_ref, wd_ref, gpost_ref, y_ref, act_ref):
    x = x_ref[0]
    h = _rms_rows(x, gpre_ref[...]).astype(BF16)
    dff = wg_ref.shape[1]
    for c0 in range(0, dff, FFN_CHUNK):
        c1 = min(c0 + FFN_CHUNK, dff)
        g = _dot(h, wg_ref[:, c0:c1])
        u = _dot(h, wu_ref[:, c0:c1])
        act_ref[:, c0:c1] = (jax.nn.silu(g) * u).astype(BF16)
    y_ref[0] = x + _rms_rows(_dot(act_ref[...], wd_ref[...]), gpost_ref[...])


def _ffn(x, wp):
    B, S, D = x.shape
    tm = TOKEN_TILE
    weights = [wp["g_ffn_pre"], wp["w_gate"], wp["w_up"], wp["w_down"], wp["g_ffn_post"]]
    dff = wp["w_gate"].shape[1]
    return pl.pallas_call(
        _ffn_kernel,
        grid=(B, S // tm),
        in_specs=[pl.BlockSpec((1, tm, D), lambda b, i: (b, i, 0))] + [_const_spec(w.shape) for w in weights],
        out_specs=pl.BlockSpec((1, tm, D), lambda b, i: (b, i, 0)),
        out_shape=jax.ShapeDtypeStruct((B, S, D), F32),
        scratch_shapes=[pltpu.VMEM((tm, dff), BF16)],
        compiler_params=_params(("arbitrary", "arbitrary")),
        name="ffn",
    )(x, *weights)


def _pad_heads_cols(w, heads, width):
    d = w.shape[0]
    w = w.reshape(d, heads, HEAD_DIM)
    w = jnp.pad(w, ((0, 0), (0, 0), (0, width - HEAD_DIM)))
    return w.reshape(d, heads * width)


def _placement(entries, rows, cols):
    m = np.zeros((rows, cols), np.float32)
    for r, c, v in entries:
        m[r, c] = v
    return jnp.asarray(m, BF16)


def _prep_layer(l, g_mix_pre, w_in, b_fgate, g_fox_out, g_chunk_out, w_out, g_mix_post, g_mem_pre, g_mem_kv,
                w_mem_q, w_mem_kv, w_mem_o, g_mem_post, g_ffn_pre, w_gate_up, w_down, g_ffn_post):
    d = w_in.shape[1]
    fw = FOX_HEADS * HEAD_DIM
    cw = CHUNK_HEADS * HEAD_DIM
    w = w_in[l]
    o = 0
    wq_f, wk_f, wv_f = w[:, o:o + fw], w[:, o + fw:o + 2 * fw], w[:, o + 2 * fw:o + 3 * fw]
    o += 3 * fw
    w_f = w[:, o:o + FOX_HEADS]
    o += FOX_HEADS
    wq_c, wk_c, wv_c = w[:, o:o + cw], w[:, o + cw:o + 2 * cw], w[:, o + 2 * cw:o + 3 * cw]
    scale = HEAD_DIM ** -0.5 * LOG2E

    row = lambda v: v.astype(F32)[None, :]
    wf3 = jnp.pad(jnp.tile(w_f, (1, N_SPLIT)), ((0, 0), (0, LANES - N_SPLIT * FOX_HEADS)))
    bf3 = jnp.pad(jnp.tile(b_fgate[l].astype(F32), N_SPLIT), (0, LANES - N_SPLIT * FOX_HEADS))[None, :]

    ek, eq, ev = [], [], []
    for h in range(FOX_HEADS):
        for i in range(N_SPLIT):
            ek.append((ONES_LANE, h * HEAD_PAD + HEAD_DIM + i, 1.0))
            ek.append((i * FOX_HEADS + h, h * HEAD_PAD + HEAD_DIM + N_SPLIT + i, -1.0))
            eq.append((h * HEAD_PAD + HEAD_DIM + i, i * FOX_HEADS + h, 1.0))
            eq.append((h * HEAD_PAD + HEAD_DIM + N_SPLIT + i, ONES_LANE, 1.0))
        ev.append((h * V_ROWS + HEAD_DIM, ONES_LANE, 1.0))
    gk = [(h * HEAD_PAD + d, h, 1.0) for h in range(FOX_HEADS) for d in range(HEAD_DIM)]

    dff = w_gate_up.shape[2] // 2
    mem_scale = (w_mem_q.shape[2] // MEM_HEADS) ** -0.5 * LOG2E
    return dict(
        g_mix_pre=row(g_mix_pre[l]),
        wf=wf3.astype(BF16), bf=bf3,
        wkf=_pad_heads_cols(wk_f, FOX_HEADS, HEAD_PAD).astype(BF16),
        wqf=_pad_heads_cols(wq_f * scale, FOX_HEADS, HEAD_PAD).T.astype(BF16),
        wvf=_pad_heads_cols(wv_f, FOX_HEADS, V_ROWS).T.astype(BF16),
        wkc=wk_c.astype(BF16),
        wqc=(wq_c * scale).T.astype(BF16),
        wvc=_pad_heads_cols(wv_c, CHUNK_HEADS, V_ROWS).T.astype(BF16),
        ek=_placement(ek, LANES, FOX_HEADS * HEAD_PAD),
        eq=_placement(eq, FOX_HEADS * HEAD_PAD, LANES),
        ev=_placement(ev, FOX_HEADS * V_ROWS, LANES),
        gk=_placement(gk, FOX_HEADS * HEAD_PAD, LANES),
        gq=_placement([(c, r, v) for r, c, v in gk], 2 * FOX_HEADS, FOX_HEADS * HEAD_PAD),
        g_fox_out=row(g_fox_out[l]), g_chunk_out=row(g_chunk_out[l]),
        w_out=w_out[l].astype(BF16), g_mix_post=row(g_mix_post[l]),
        g_mem_pre=row(g_mem_pre[l]), g_mem_kv=row(g_mem_kv[l]),
        w_mem_q=(w_mem_q[l] * mem_scale).astype(BF16), w_mem_kv=w_mem_kv[l].astype(BF16),
        w_mem_o=w_mem_o[l].astype(BF16), g_mem_post=row(g_mem_post[l]),
        g_ffn_pre=row(g_ffn_pre[l]),
        w_gate=w_gate_up[l][:, :dff].astype(BF16), w_up=w_gate_up[l][:, dff:].astype(BF16),
        w_down=w_down[l].astype(BF16), g_ffn_post=row(g_ffn_post[l]),
    )


def kernel(x, mem, g_mix_pre, w_in, b_fgate, rel_bias, g_fox_out, g_chunk_out, w_out, g_mix_post, g_mem_pre,
           g_mem_kv, w_mem_q, w_mem_kv, w_mem_o, g_mem_post, g_ffn_pre, w_gate_up, w_down, g_ffn_post):
    depth = w_in.shape[0]
    for l in range(depth):
        wp = _prep_layer(l, g_mix_pre, w_in, b_fgate, g_fox_out, g_chunk_out, w_out, g_mix_post, g_mem_pre,
                         g_mem_kv, w_mem_q, w_mem_kv, w_mem_o, g_mem_post, g_ffn_pre, w_gate_up, w_down,
                         g_ffn_post)
        kf, qf, vf, kc, qc, vc, kstats, qstats = _inproj(x, wp["g_mix_pre"], wp)
        o_f = _fox(kf, qf, vf, kstats, qstats)
        o_c = _chunk(kc, qc, vc, _chunk_bias_blocks(rel_bias[l]))
        kv = _memkv(mem, wp["g_mem_kv"], wp["w_mem_kv"])
        x = _mix_mem(x, o_f, o_c, kv, wp)
        x = _ffn(x, wp)
    return x
```

```python
import functools

import jax
import jax.numpy as jnp
import numpy as np
from jax import lax
from jax.experimental import pallas as pl
from jax.experimental.pallas import tpu as pltpu

F32 = jnp.float32
BF16 = jnp.bfloat16

HEAD_DIM = 64
FOX_HEADS = 8
CHUNK_HEADS = 8
CHUNK = 64
LEFT_CHUNKS = 8
REL_CLIP = 128
MEM_HEADS = 4
EPS = 1e-6
NEG_INF = -1e30

LANES = 128
HEAD_PAD = 128
V_ROWS = 80
Q_ROWS = 80
ONES_LANE = LANES - 1
N_SPLIT = 3
TOKEN_TILE = 512
FOX_BLOCK = 1024
FOX_CHUNKS = 4
PV_LAG = 2
MIX_GROUPS = 2
LOG2E = 1.4426950408889634
SKIP_LOG2 = 150.0
NORM_MARGIN = 1.02
BOUND_LIMIT = 100.0
VMEM_LIMIT = 56 * 1024 * 1024


def _params(semantics):
    return pltpu.CompilerParams(dimension_semantics=semantics, vmem_limit_bytes=VMEM_LIMIT)


def _const_spec(shape):
    nd = len(shape)
    return pl.BlockSpec(shape, lambda *_: (0,) * nd, pipeline_mode=pl.Buffered(1))


def _rms_rows(x, g):
    ms = jnp.mean(x * x, axis=-1, keepdims=True)
    return x * lax.rsqrt(ms + EPS) * g


def _split3_lanes(v):
    lane = lax.broadcasted_iota(jnp.int32, v.shape, 1)
    t1 = v.astype(BF16)
    r1 = v - t1.astype(F32)
    t2 = r1.astype(BF16)
    r2 = r1 - t2.astype(F32)
    t3 = r2.astype(BF16)
    return jnp.where(lane < FOX_HEADS, t1, jnp.where(lane < 2 * FOX_HEADS, t2, t3))


def _dot(a, b):
    return jnp.dot(a, b, preferred_element_type=F32)


def _dot_nt(a, b):
    return lax.dot_general(a, b, (((1,), (1,)), ((), ())), preferred_element_type=F32)


def _inproj_kernel(x_ref, g_ref, wf_ref, bf_ref, wkf_ref, wqf_ref, wvf_ref, wkc_ref, wqc_ref, wvc_ref,
                   ek_ref, eq_ref, ones_ref, gk_ref, gq_ref,
                   kf_ref, qf_ref, vf_ref, kc_ref, qc_ref, vc_ref, ks_ref, qs_ref, qn_ref, cr_ref, carry_ref):
    tm = x_ref.shape[1]

    @pl.when(pl.program_id(1) == 0)
    def _():
        carry_ref[...] = jnp.zeros_like(carry_ref)

    h = _rms_rows(x_ref[0], g_ref[...]).astype(BF16)

    z = _dot(h, wf_ref[...]) + bf_ref[...]

    ones_rows = jnp.tile(ones_ref[...], (1, tm // LANES))
    vf_ref[0, 0] = (_dot_nt(wvf_ref[...], h) + ones_rows).astype(BF16)
    kc_ref[0, 0] = _dot(h, wkc_ref[...]).astype(BF16)
    qc_ref[0] = _dot_nt(wqc_ref[...], h).astype(BF16)
    vc_ref[0, 0] = (_dot_nt(wvc_ref[...], h) + ones_rows).astype(BF16)
    k_pairs = _dot(h, wkf_ref[...]).astype(BF16).astype(F32)
    q_rows = _dot_nt(wqf_ref[...], h)

    lf = jnp.minimum(z, 0.0) - jnp.log1p(jnp.exp(-jnp.abs(z)))
    row = lax.broadcasted_iota(jnp.int32, (tm, tm), 0)
    col = lax.broadcasted_iota(jnp.int32, (tm, tm), 1)
    tri = (col <= row).astype(BF16)
    l1 = lf.astype(BF16)
    l2 = (lf - l1.astype(F32)).astype(BF16)
    c = _dot(tri, l1) + _dot(tri, l2) + carry_ref[...]
    carry_ref[...] = c[tm - 1:tm, :]

    lane = lax.broadcasted_iota(jnp.int32, c.shape, 1)
    gate = jnp.where(lane == ONES_LANE, jnp.ones_like(c).astype(BF16), _split3_lanes(c * LOG2E))

    k_gate = _dot(gate, ek_ref[...])
    feature_lane = lax.broadcasted_iota(jnp.int32, (tm, HEAD_PAD), 1) < HEAD_DIM
    k_heads = []
    for hh in range(FOX_HEADS):
        group = slice((hh // 2) * HEAD_PAD, (hh // 2 + 1) * HEAD_PAD)
        pair, pair_gate = k_pairs[:, group], k_gate[:, group]
        if hh % 2:
            pair, pair_gate = pltpu.roll(pair, HEAD_DIM, 1), pltpu.roll(pair_gate, HEAD_DIM, 1)
        k_heads.append(jnp.where(feature_lane, pair, pair_gate))
    kf_ref[0, 0] = jnp.concatenate(k_heads, axis=1).astype(BF16)
    qf = (q_rows + _dot_nt(eq_ref[...], gate)).astype(BF16)
    qf_ref[0] = qf

    qf32 = qf.astype(F32)
    k_norm2 = jnp.max(_dot((k_pairs * k_pairs).astype(BF16), gk_ref[...]), axis=0, keepdims=True)
    q_norm2_rows = _dot(gq_ref[...], (qf32 * qf32).astype(BF16))
    q_norm2 = jnp.max(q_norm2_rows, axis=1, keepdims=True)
    c2 = c * LOG2E
    srow = lax.broadcasted_iota(jnp.int32, ks_ref.shape[2:], 0)
    ks_ref[0, 0] = jnp.where(srow == 0, c2[tm - 1:tm, :], jnp.where(srow == 1, c2[0:1, :], k_norm2))
    qs_ref[0, 0] = jnp.broadcast_to(q_norm2, qs_ref.shape[2:])
    qn_ref[0] = q_norm2_rows
    cr_ref[0] = jnp.transpose(c2)[:qn_ref.shape[1], :]


def _inproj(x, g, wp):
    B, S, D = x.shape
    tm = TOKEN_TILE
    nk = S // tm
    hp = FOX_HEADS * HEAD_PAD
    hc = CHUNK_HEADS * HEAD_DIM
    hv = FOX_HEADS * V_ROWS
    tile = lambda b, i: (b, i, 0)
    out_shapes = (
        jax.ShapeDtypeStruct((B, nk, tm, hp), BF16),
        jax.ShapeDtypeStruct((B, FOX_HEADS * Q_ROWS, S), BF16),
        jax.ShapeDtypeStruct((B, nk, hv, tm), BF16),
        jax.ShapeDtypeStruct((B, nk, tm, hc), BF16),
        jax.ShapeDtypeStruct((B, hc, S), BF16),
        jax.ShapeDtypeStruct((B, nk, hv, tm), BF16),
        jax.ShapeDtypeStruct((B, nk, 8, LANES), F32),
        jax.ShapeDtypeStruct((B, nk, 2 * FOX_HEADS, LANES), F32),
        jax.ShapeDtypeStruct((B, 2 * FOX_HEADS, S), F32),
        jax.ShapeDtypeStruct((B, 2 * FOX_HEADS, S), F32),
    )
    stat_spec = lambda r: pl.BlockSpec((1, 1, r, LANES), lambda b, i: (b, i, 0, 0))
    k_spec = lambda w: pl.BlockSpec((1, 1, tm, w), lambda b, i: (b, i, 0, 0))
    q_spec = lambda w: pl.BlockSpec((1, w, tm), lambda b, i: (b, 0, i))
    v_spec = pl.BlockSpec((1, 1, hv, tm), lambda b, i: (b, i, 0, 0))
    weights = [g, wp["wf"], wp["bf"], wp["wkf"], wp["wqf"], wp["wvf"], wp["wkc"], wp["wqc"], wp["wvc"],
               wp["ek"], wp["eq"], wp["ones_rows"], wp["gk"], wp["gq"]]
    return pl.pallas_call(
        _inproj_kernel,
        grid=(B, nk),
        in_specs=[pl.BlockSpec((1, tm, D), tile)] + [_const_spec(w.shape) for w in weights],
        out_specs=(k_spec(hp), q_spec(FOX_HEADS * Q_ROWS), v_spec, k_spec(hc), q_spec(hc), v_spec,
                   stat_spec(8), stat_spec(2 * FOX_HEADS), q_spec(2 * FOX_HEADS), q_spec(2 * FOX_HEADS)),
        out_shape=out_shapes,
        scratch_shapes=[pltpu.VMEM((1, LANES), F32)],
        compiler_params=_params(("arbitrary", "arbitrary")),
        name="inproj",
    )(x, *weights)


def _fox_kernel(thr_ref, ck_ref, kn_ref, slow_ref, q_ref, qnext_ref, qn2_ref, crow_ref, k_ref, v_ref, o_ref,
                s0_ref, s1_ref, mx0_ref, mx1_ref, sf_ref, mxf_ref, acc_ref, m_ref):
    iq = pl.program_id(2)
    nq = pl.num_programs(2)
    tq = q_ref.shape[2]
    nkb, tk = k_ref.shape[1:3]
    tv = v_ref.shape[3]
    def contraction_rows(q):
        return jnp.concatenate([q, jnp.zeros((k_ref.shape[3] - q.shape[0], tq), q.dtype)], axis=0)

    qT = contraction_rows(q_ref[0])
    m_ref[...] = jnp.full(m_ref.shape, NEG_INF, F32)
    acc_ref[...] = jnp.zeros_like(acc_ref)
    cur = lax.bitwise_and(iq, 1)
    nxt = 1 - cur
    head = pl.program_id(0) * pl.num_programs(1) + pl.program_id(1)

    def first_block(i):
        thr = thr_ref[head * nq + i]
        count = jnp.int32(0)
        for j in range(nkb):
            skip = jnp.logical_and(j < i, thr < ck_ref[head * nkb + j])
            count = count + jnp.where(skip, 1, 0)
        return count

    j0 = first_block(iq)

    ckeys = tk // FOX_CHUNKS
    chunk_rows = [slice(c * ckeys, (c + 1) * ckeys) for c in range(FOX_CHUNKS)]

    def causal(c, s):
        key = lax.broadcasted_iota(jnp.int32, (ckeys, tq), 0) + c * ckeys
        qry = lax.broadcasted_iota(jnp.int32, (ckeys, tq), 1)
        return jnp.where(key <= qry, s, NEG_INF)

    def diagonal_max(s_ref):
        return functools.reduce(jnp.maximum, [jnp.max(causal(c, s_ref[rows, :]), axis=0, keepdims=True)
                                              for c, rows in enumerate(chunk_rows)])

    def step(jn, q_n, sn_ref, mxn_ref, jc=None, sc_ref=None, mx_c=None, diagonal=False):
        if jc is not None:
            m_old = m_ref[...]
            m_new = jnp.maximum(m_old, mx_c)
        mx_parts, pv = [], None
        for c, rows in enumerate(chunk_rows):
            if jc is not None:
                s_c = sc_ref[rows, :]
                p = jnp.exp2((causal(c, s_c) if diagonal else s_c) - m_new).astype(BF16)
            s = _dot(k_ref[0, jn, rows, :], q_n)
            sn_ref[rows, :] = s
            mx_parts.append(jnp.max(s, axis=0, keepdims=True))
            if jc is not None:
                off = (c * ckeys) % tv
                part = _dot(v_ref[0, jc * (tk // tv) + (c * ckeys) // tv, :, off:off + ckeys], p)
                pv = part if pv is None else pv + part
        mxn_ref[...] = functools.reduce(jnp.maximum, mx_parts)
        if jc is not None:
            acc_ref[...] = jnp.exp2(m_old - m_new) * acc_ref[...] + pv
            m_ref[...] = m_new

    def diagonal_and_handover(s_ref):
        step(first_block(jnp.minimum(iq + 1, nq - 1)), contraction_rows(qnext_ref[0]), sf_ref.at[nxt], mxf_ref.at[nxt],
             iq, s_ref, diagonal_max(s_ref), diagonal=True)

    def exact_max_path():
        @pl.when(iq == 0)
        def _():
            step(0, qT, sf_ref.at[cur], mxf_ref.at[cur])

        @pl.when(j0 == iq)
        def _():
            diagonal_and_handover(sf_ref.at[cur])

        @pl.when(j0 < iq)
        def _():
            step(j0 + 1, qT, s1_ref, mx1_ref, j0, sf_ref.at[cur], mxf_ref[cur])
            rest = iq - j0 - 1

            def pair(t, carry):
                j = j0 + 2 * t + 1
                step(j + 1, qT, s0_ref, mx0_ref, j, s1_ref, mx1_ref[...])
                step(j + 2, qT, s1_ref, mx1_ref, j + 1, s0_ref, mx0_ref[...])
                return carry

            lax.fori_loop(0, lax.shift_right_logical(rest, 1), pair, 0)
            odd = lax.bitwise_and(rest, 1)

            @pl.when(odd == 0)
            def _():
                diagonal_and_handover(s1_ref)

            @pl.when(odd == 1)
            def _():
                step(iq, qT, s0_ref, mx0_ref, iq - 1, s1_ref, mx1_ref[...])
                diagonal_and_handover(s0_ref)

    def bound_path():
        q_norm = jnp.sqrt(qn2_ref[0, 0]) * NORM_MARGIN
        c_t = crow_ref[0, 0]

        def pv_chunk(j, c, p):
            off = (c * ckeys) % tv
            return _dot(v_ref[0, j * (tk // tv) + (c * ckeys) // tv, :, off:off + ckeys], p)

        def run_blocks(blocks):
            m = m_ref[...]
            stabiliser = []
            for _, bound, _ in blocks:
                m_new = jnp.maximum(m, bound)
                stabiliser.append((m, m_new))
                m = m_new
            m_ref[...] = m
            stream = [(b, c) for b in range(len(blocks)) for c in range(FOX_CHUNKS)]
            weights, pv = {}, [None] * len(blocks)

            def finish(b, c):
                w, q0 = weights.pop((b, c))
                part = pv_chunk(blocks[b][0], c, w)
                if q0:
                    part = jnp.concatenate([jnp.zeros((part.shape[0], q0), F32), part], axis=1)
                pv[b] = part if pv[b] is None else pv[b] + part
                if c == FOX_CHUNKS - 1:
                    m_old, m_new = stabiliser[b]
                    acc_ref[...] = jnp.exp2(m_old - m_new) * acc_ref[...] + pv[b]

            for i, (b, c) in enumerate(stream):
                j, _, diagonal = blocks[b]
                q0 = c * ckeys if diagonal else 0
                s = _dot(k_ref[0, j, chunk_rows[c], :], qT[:, q0:])
                if diagonal:
                    key = lax.broadcasted_iota(jnp.int32, s.shape, 0)
                    qry = lax.broadcasted_iota(jnp.int32, s.shape, 1)
                    s = jnp.where(key <= qry, s, NEG_INF)
                weights[b, c] = jnp.exp2(s - stabiliser[b][1][:, q0:]).astype(BF16), q0
                if i >= PV_LAG:
                    finish(*stream[i - PV_LAG])
            for item in stream[-PV_LAG:]:
                finish(*item)

        def earlier(j):
            return j, q_norm * kn_ref[head * nkb + j] + (c_t - ck_ref[head * nkb + j]), False

        diagonal = (iq, q_norm * kn_ref[head * nkb + iq], True)

        def pair(t, carry):
            run_blocks([earlier(j0 + 2 * t), earlier(j0 + 2 * t + 1)])
            return carry

        n_earlier = iq - j0
        lax.fori_loop(0, lax.shift_right_logical(n_earlier, 1), pair, 0)
        odd = lax.bitwise_and(n_earlier, 1)

        @pl.when(odd == 0)
        def _():
            run_blocks([diagonal])

        @pl.when(odd == 1)
        def _():
            run_blocks([earlier(iq - 1), diagonal])

    use_bound = slow_ref[head] == 0
    pl.when(use_bound)(bound_path)
    pl.when(jnp.logical_not(use_bound))(exact_max_path)

    acc = acc_ref[...]
    o_ref[0] = (acc[:HEAD_DIM] / acc[HEAD_DIM:HEAD_DIM + 1]).astype(o_ref.dtype)


def _fox_skip_tables(kstats, qstats):
    r = FOX_BLOCK // TOKEN_TILE
    B, nt = kstats.shape[:2]
    H = FOX_HEADS
    per_block = lambda a: a.reshape(B, nt // r, r, H)
    ck = per_block(kstats[:, :, 0, :H])[:, :, r - 1]
    cq = per_block(kstats[:, :, 1, :H])[:, :, 0]
    k_norm = jnp.sqrt(jnp.max(kstats[:, :, 2, :H], axis=1))
    q_norm = jnp.sqrt(jnp.max(per_block(qstats[:, :, :H, 0]), axis=2))
    gap = 2.0 * NORM_MARGIN * q_norm * k_norm[:, None, :]
    thr = cq + gap + SKIP_LOG2
    kn = jnp.sqrt(jnp.max(per_block(kstats[:, :, 2, :H]), axis=2)) * NORM_MARGIN
    slow = jnp.logical_not(jnp.max(gap, axis=1) < BOUND_LIMIT).astype(jnp.int32)
    flat = lambda a: jnp.transpose(a, (0, 2, 1)).reshape(-1).astype(F32)
    return flat(thr), flat(ck), flat(kn), slow.reshape(-1)


def _fox(kf, qf, vf, kstats, qstats, q_norm2, c_rows):
    B, nv, tv, hp = kf.shape
    S = qf.shape[2]
    H = hp // HEAD_PAD
    tq = tk = FOX_BLOCK
    kf = kf.reshape(B, S // tk, tk, hp)
    nq = S // tq
    tables = _fox_skip_tables(kstats, qstats)
    per_query = lambda a: a.reshape(B, a.shape[1], 1, S)
    row_spec = pl.BlockSpec((1, 1, 1, tq), lambda b, h, i, *_: (b, h, 0, i))
    grid_spec = pltpu.PrefetchScalarGridSpec(
        num_scalar_prefetch=len(tables),
        grid=(B, H, nq),
        in_specs=[
            pl.BlockSpec((1, Q_ROWS, tq), lambda b, h, i, *_: (b, h, i)),
            pl.BlockSpec((1, Q_ROWS, tq), lambda b, h, i, *_: (b, h, jnp.minimum(i + 1, nq - 1))),
            row_spec,
            row_spec,
            pl.BlockSpec((1, S // tk, tk, HEAD_PAD), lambda b, h, i, *_: (b, 0, 0, h)),
            pl.BlockSpec((1, nv, V_ROWS, tv), lambda b, h, i, *_: (b, 0, h, 0)),
        ],
        out_specs=pl.BlockSpec((1, HEAD_DIM, tq), lambda b, h, i, *_: (b, h, i)),
        scratch_shapes=[pltpu.VMEM((tk, tq), F32), pltpu.VMEM((tk, tq), F32),
                        pltpu.VMEM((1, tq), F32), pltpu.VMEM((1, tq), F32),
                        pltpu.VMEM((2, tk, tq), F32), pltpu.VMEM((2, 1, tq), F32),
                        pltpu.VMEM((V_ROWS, tq), F32), pltpu.VMEM((1, tq), F32)],
    )
    return pl.pallas_call(
        _fox_kernel,
        grid_spec=grid_spec,
        out_shape=jax.ShapeDtypeStruct((B, H * HEAD_DIM, S), BF16),
        compiler_params=_params(("arbitrary", "arbitrary", "arbitrary")),
        name="fox",
    )(*tables, qf, qf, per_query(q_norm2), per_query(c_rows), kf, vf)


BAND_BLOCKS = LEFT_CHUNKS * CHUNK // LANES


def _bias_block_id(delta):
    if delta < 0 or delta > BAND_BLOCKS:
        return None
    return {0: 0, 1: 1, BAND_BLOCKS: 3}.get(delta, 2)


def _chunk_kernel(q_ref, kp_ref, kc_ref, vp_ref, vc_ref, blk_ref, o_ref):
    iq = pl.program_id(1)
    tq = q_ref.shape[2]
    nb = tq // LANES
    before_start = jnp.where(iq > 0, 0.0, NEG_INF).astype(F32)
    zero_blk = jnp.zeros((LANES, LANES), BF16)
    row = lax.broadcasted_iota(jnp.int32, (2 * HEAD_DIM, tq), 0)

    def scores(h):
        pair, odd = divmod(h, 2)
        lanes = slice(pair * 2 * HEAD_DIM, (pair + 1) * 2 * HEAD_DIM)
        q_pair = q_ref[0, lanes, :]
        qT = jnp.where((row >= HEAD_DIM) == bool(odd), q_pair, jnp.zeros_like(q_pair))
        return _dot(kp_ref[0, 0, :, lanes], qT), _dot(kc_ref[0, 0, :, lanes], qT)

    ahead = [scores(0), scores(1)]
    for h in range(CHUNK_HEADS):
        s_half = ahead.pop(0)
        if h + 2 < CHUNK_HEADS:
            ahead.append(scores(h + 2))
        p_blocks = {}
        for b in range(nb):
            cols = slice(b * LANES, (b + 1) * LANES)
            entries = []
            for half in (0, 1):
                for al in range(nb):
                    bid = _bias_block_id(b - (al - nb * (1 - half)))
                    if bid is None:
                        continue
                    sb = s_half[half][al * LANES:(al + 1) * LANES, cols] + blk_ref[h, bid]
                    if half == 0:
                        sb = sb + before_start
                    entries.append((half, al, sb))
            m = functools.reduce(jnp.maximum, [jnp.max(sb, axis=0, keepdims=True) for _, _, sb in entries])
            for half, al, sb in entries:
                p_blocks[half, al, b] = jnp.exp2(sb - m).astype(BF16)
        p_half = [jnp.concatenate([jnp.concatenate([p_blocks.get((half, al, b), zero_blk) for b in range(nb)],
                                                   axis=1) for al in range(nb)], axis=0) for half in (0, 1)]
        vrows = slice(h * V_ROWS, (h + 1) * V_ROWS)
        acc = _dot(vc_ref[0, 0, vrows, :], p_half[1]) + _dot(vp_ref[0, 0, vrows, :], p_half[0])
        o_ref[0, h * HEAD_DIM:(h + 1) * HEAD_DIM, :] = (
            acc[:HEAD_DIM] / acc[HEAD_DIM:HEAD_DIM + 1]).astype(o_ref.dtype)


def _chunk(kc, qc, vc, blocks):
    B, nk, tk, hc = kc.shape
    S = qc.shape[2]
    hv = vc.shape[2]
    tq = tk
    assert tq == LEFT_CHUNKS * CHUNK and 2 * CHUNK == LANES
    prev = lambda i: jnp.maximum(i - 1, 0)
    return pl.pallas_call(
        _chunk_kernel,
        grid=(B, S // tq),
        in_specs=[
            pl.BlockSpec((1, hc, tq), lambda b, i: (b, 0, i)),
            pl.BlockSpec((1, 1, tk, hc), lambda b, i: (b, prev(i), 0, 0)),
            pl.BlockSpec((1, 1, tk, hc), lambda b, i: (b, i, 0, 0)),
            pl.BlockSpec((1, 1, hv, tk), lambda b, i: (b, prev(i), 0, 0)),
            pl.BlockSpec((1, 1, hv, tk), lambda b, i: (b, i, 0, 0)),
            _const_spec(blocks.shape),
        ],
        out_specs=pl.BlockSpec((1, hc, tq), lambda b, i: (b, 0, i)),
        out_shape=jax.ShapeDtypeStruct((B, hc, S), BF16),
        compiler_params=_params(("arbitrary", "arbitrary")),
        name="chunk",
    )(qc, kc, kc, vc, vc, blocks)


def _chunk_bias_blocks(rel_bias):
    heads = rel_bias.shape[0]
    n = LANES
    span = 3 * n - 1
    dist = np.arange(span) - (n - 1)
    by_dist = rel_bias.astype(F32)[:, np.clip(dist, -(CHUNK - 1), REL_CLIP) + (CHUNK - 1)] * LOG2E
    padded = jnp.pad(by_dist, ((0, 0), (0, 1)))
    skew = jnp.broadcast_to(padded[:, None, :], (heads, n, span + 1)).reshape(heads, n * (span + 1))
    near = skew[:, :n * span].reshape(heads, n, span)[:, :, n - 1:]
    key_chunk = (np.arange(n) // CHUNK)[:, None]
    qry_chunk = (np.arange(n) // CHUNK)[None, :]
    clipped = jnp.broadcast_to(by_dist[:, -1][:, None, None], (heads, n, n))
    blocks = [
        jnp.where((qry_chunk >= key_chunk)[None], near[:, :, :n], NEG_INF),
        near[:, :, n:],
        clipped,
        jnp.where((2 * BAND_BLOCKS + qry_chunk - key_chunk <= LEFT_CHUNKS)[None], clipped, NEG_INF),
    ]
    return jnp.stack(blocks, axis=1)


def _memkv_kernel(mem_ref, g_ref, w_ref, kv_ref):
    m = _rms_rows(mem_ref[0], g_ref[...]).astype(BF16)
    kv_ref[0] = _dot(m, w_ref[...]).astype(BF16)


def _memkv(mem, g, w):
    B, N, D = mem.shape
    return pl.pallas_call(
        _memkv_kernel,
        grid=(B,),
        in_specs=[pl.BlockSpec((1, N, D), lambda b: (b, 0, 0)), _const_spec(g.shape), _const_spec(w.shape)],
        out_specs=pl.BlockSpec((1, N, 2 * D), lambda b: (b, 0, 0)),
        out_shape=jax.ShapeDtypeStruct((B, N, 2 * D), BF16),
        compiler_params=_params(("arbitrary",)),
        name="memkv",
    )(mem, g, w)


def _mix_mem_kernel(x_ref, of_ref, oc_ref, kv_ref, gfo_ref, gco_ref, wout_ref, gmp_ref,
                    gmq_ref, wq_ref, wo_ref, gmo_ref, y_ref):
    tm, d = x_ref.shape[1:]
    dh = d // MEM_HEADS
    kv = kv_ref[0]
    groups = [slice(r * tm // MIX_GROUPS, (r + 1) * tm // MIX_GROUPS) for r in range(MIX_GROUPS)]

    def mixer_input(rows):
        of = jnp.transpose(of_ref[0, :, rows].astype(F32))
        oc = jnp.transpose(oc_ref[0, :, rows].astype(F32))
        return jnp.concatenate([_rms_rows(of, gfo_ref[...]), _rms_rows(oc, gco_ref[...])], axis=-1).astype(BF16)

    def cross_attention(q):
        outs = []
        for hh in range(MEM_HEADS):
            sl = slice(hh * dh, (hh + 1) * dh)
            s = _dot_nt(q[:, sl], kv[:, sl])
            p = jnp.exp2(s - jnp.max(s, axis=-1, keepdims=True))
            l = jnp.sum(p, axis=-1, keepdims=True)
            outs.append(_dot(p.astype(BF16), kv[:, d + hh * dh:d + (hh + 1) * dh]) / l)
        return jnp.concatenate(outs, axis=-1).astype(BF16)

    cat = [mixer_input(rows) for rows in groups]
    x1 = [x_ref[0, rows, :] + _rms_rows(_dot(c, wout_ref[...]), gmp_ref[...]) for rows, c in zip(groups, cat)]
    q = [_dot(_rms_rows(v, gmq_ref[...]).astype(BF16), wq_ref[...]).astype(BF16) for v in x1]
    ca = [cross_attention(v) for v in q]
    for rows, v, c in zip(groups, x1, ca):
        y_ref[0, rows, :] = v + _rms_rows(_dot(c, wo_ref[...]), gmo_ref[...])


def _mix_mem(x, of, oc, kv, wp):
    B, S, D = x.shape
    tm = MIX_GROUPS * TOKEN_TILE
    half = of.shape[1]
    weights = [wp["g_fox_out"], wp["g_chunk_out"], wp["w_out"], wp["g_mix_post"],
               wp["g_mem_pre"], wp["w_mem_q"], wp["w_mem_o"], wp["g_mem_post"]]
    return pl.pallas_call(
        _mix_mem_kernel,
        grid=(B, S // tm),
        in_specs=[
            pl.BlockSpec((1, tm, D), lambda b, i: (b, i, 0)),
            pl.BlockSpec((1, half, tm), lambda b, i: (b, 0, i)),
            pl.BlockSpec((1, half, tm), lambda b, i: (b, 0, i)),
            pl.BlockSpec((1,) + kv.shape[1:], lambda b, i: (b, 0, 0)),
        ] + [_const_spec(w.shape) for w in weights],
        out_specs=pl.BlockSpec((1, tm, D), lambda b, i: (b, i, 0)),
        out_shape=jax.ShapeDtypeStruct((B, S, D), F32),
        compiler_params=_params(("arbitrary", "arbitrary")),
        name="mix_mem",
    )(x, of, oc, kv, *weights)


FFN_CHUNK = 1024


def _ffn_kernel(x_ref, gpre_ref, wg_ref, wu_ref, wd_ref, gpost_ref, y_ref, act_ref):
    x = x_ref[0]
    h = _rms_rows(x, gpre_ref[...]).astype(BF16)
    dff = wg_ref.shape[1]
    for c0 in range(0, dff, FFN_CHUNK):
        c1 = min(c0 + FFN_CHUNK, dff)
        g = _dot(h, wg_ref[:, c0:c1])
        u = _dot(h, wu_ref[:, c0:c1])
        act_ref[:, c0:c1] = (jax.nn.silu(g) * u).astype(BF16)
    y_ref[0] = x + _rms_rows(_dot(act_ref[...], wd_ref[...]), gpost_ref[...])


def _ffn(x, wp):
    B, S, D = x.shape
    tm = TOKEN_TILE
    weights = [wp["g_ffn_pre"], wp["w_gate"], wp["w_up"], wp["w_down"], wp["g_ffn_post"]]
    dff = wp["w_gate"].shape[1]
    return pl.pallas_call(
        _ffn_kernel,
        grid=(B, S // tm),
        in_specs=[pl.BlockSpec((1, tm, D), lambda b, i: (b, i, 0))] + [_const_spec(w.shape) for w in weights],
        out_specs=pl.BlockSpec((1, tm, D), lambda b, i: (b, i, 0)),
        out_shape=jax.ShapeDtypeStruct((B, S, D), F32),
        scratch_shapes=[pltpu.VMEM((tm, dff), BF16)],
        compiler_params=_params(("arbitrary", "arbitrary")),
        name="ffn",
    )(x, *weights)


def _pad_heads_cols(w, heads, width):
    d = w.shape[0]
    w = w.reshape(d, heads, HEAD_DIM)
    w = jnp.pad(w, ((0, 0), (0, 0), (0, width - HEAD_DIM)))
    return w.reshape(d, heads * width)


def _placement(entries, rows, cols):
    m = np.zeros((rows, cols), np.float32)
    for r, c, v in entries:
        m[r, c] = v
    return jnp.asarray(m, BF16)


def _prep_layer(l, g_mix_pre, w_in, b_fgate, g_fox_out, g_chunk_out, w_out, g_mix_post, g_mem_pre, g_mem_kv,
                w_mem_q, w_mem_kv, w_mem_o, g_mem_post, g_ffn_pre, w_gate_up, w_down, g_ffn_post):
    d = w_in.shape[1]
    fw = FOX_HEADS * HEAD_DIM
    cw = CHUNK_HEADS * HEAD_DIM
    w = w_in[l]
    o = 0
    wq_f, wk_f, wv_f = w[:, o:o + fw], w[:, o + fw:o + 2 * fw], w[:, o + 2 * fw:o + 3 * fw]
    o += 3 * fw
    w_f = w[:, o:o + FOX_HEADS]
    o += FOX_HEADS
    wq_c, wk_c, wv_c = w[:, o:o + cw], w[:, o + cw:o + 2 * cw], w[:, o + 2 * cw:o + 3 * cw]
    scale = HEAD_DIM ** -0.5 * LOG2E

    row = lambda v: v.astype(F32)[None, :]
    wf3 = jnp.pad(jnp.tile(w_f, (1, N_SPLIT)), ((0, 0), (0, LANES - N_SPLIT * FOX_HEADS)))
    bf3 = jnp.pad(jnp.tile(b_fgate[l].astype(F32), N_SPLIT), (0, LANES - N_SPLIT * FOX_HEADS))[None, :]

    ek, eq = [], []
    for h in range(FOX_HEADS):
        for i in range(N_SPLIT):
            gate_lane = (h // 2) * HEAD_PAD + (HEAD_DIM if h % 2 == 0 else 0)
            ek.append((ONES_LANE, gate_lane + i, 1.0))
            ek.append((i * FOX_HEADS + h, gate_lane + N_SPLIT + i, -1.0))
            eq.append((h * Q_ROWS + HEAD_DIM + i, i * FOX_HEADS + h, 1.0))
            eq.append((h * Q_ROWS + HEAD_DIM + N_SPLIT + i, ONES_LANE, 1.0))
    gk = [(h * HEAD_DIM + d, h, 1.0) for h in range(FOX_HEADS) for d in range(HEAD_DIM)]
    gq = [(h, h * Q_ROWS + d, 1.0) for h in range(FOX_HEADS) for d in range(HEAD_DIM)]
    ones_rows = np.zeros((FOX_HEADS * V_ROWS, LANES), np.float32)
    ones_rows[np.arange(FOX_HEADS) * V_ROWS + HEAD_DIM] = 1.0

    dff = w_gate_up.shape[2] // 2
    mem_scale = (w_mem_q.shape[2] // MEM_HEADS) ** -0.5 * LOG2E
    return dict(
        g_mix_pre=row(g_mix_pre[l]),
        wf=wf3.astype(BF16), bf=bf3,
        wkf=wk_f.astype(BF16),
        wqf=_pad_heads_cols(wq_f * scale, FOX_HEADS, Q_ROWS).T.astype(BF16),
        wvf=_pad_heads_cols(wv_f, FOX_HEADS, V_ROWS).T.astype(BF16),
        wkc=wk_c.astype(BF16),
        wqc=(wq_c * scale).T.astype(BF16),
        wvc=_pad_heads_cols(wv_c, CHUNK_HEADS, V_ROWS).T.astype(BF16),
        ek=_placement(ek, LANES, FOX_HEADS * HEAD_DIM),
        eq=_placement(eq, FOX_HEADS * Q_ROWS, LANES),
        ones_rows=jnp.asarray(ones_rows),
        gk=_placement(gk, FOX_HEADS * HEAD_DIM, LANES),
        gq=_placement(gq, 2 * FOX_HEADS, FOX_HEADS * Q_ROWS),
        g_fox_out=row(g_fox_out[l]), g_chunk_out=row(g_chunk_out[l]),
        w_out=w_out[l].astype(BF16), g_mix_post=row(g_mix_post[l]),
        g_mem_pre=row(g_mem_pre[l]), g_mem_kv=row(g_mem_kv[l]),
        w_mem_q=(w_mem_q[l] * mem_scale).astype(BF16), w_mem_kv=w_mem_kv[l].astype(BF16),
        w_mem_o=w_mem_o[l].astype(BF16), g_mem_post=row(g_mem_post[l]),
        g_ffn_pre=row(g_ffn_pre[l]),
        w_gate=w_gate_up[l][:, :dff].astype(BF16), w_up=w_gate_up[l][:, dff:].astype(BF16),
        w_down=w_down[l].astype(BF16), g_ffn_post=row(g_ffn_post[l]),
    )


def kernel(x, mem, g_mix_pre, w_in, b_fgate, rel_bias, g_fox_out, g_chunk_out, w_out, g_mix_post, g_mem_pre,
           g_mem_kv, w_mem_q, w_mem_kv, w_mem_o, g_mem_post, g_ffn_pre, w_gate_up, w_down, g_ffn_post):
    depth = w_in.shape[0]
    for l in range(depth):
        wp = _prep_layer(l, g_mix_pre, w_in, b_fgate, g_fox_out, g_chunk_out, w_out, g_mix_post, g_mem_pre,
                         g_mem_kv, w_mem_q, w_mem_kv, w_mem_o, g_mem_post, g_ffn_pre, w_gate_up, w_down,
                         g_ffn_post)
        kf, qf, vf, kc, qc, vc, kstats, qstats, q_norm2, c_rows = _inproj(x, wp["g_mix_pre"], wp)
        o_f = _fox(kf, qf, vf, kstats, qstats, q_norm2, c_rows)
        o_c = _chunk(kc, qc, vc, _chunk_bias_blocks(rel_bias[l]))
        kv = _memkv(mem, wp["g_mem_kv"], wp["w_mem_kv"])
        x = _mix_mem(x, o_f, o_c, kv, wp)
        x = _ffn(x, wp)
    return x
```

```python
import functools

import jax
import jax.numpy as jnp
import numpy as np
from jax import lax
from jax.experimental import pallas as pl
from jax.experimental.pallas import tpu as pltpu

F32 = jnp.float32
BF16 = jnp.bfloat16

HEAD_DIM = 64
FOX_HEADS = 8
CHUNK_HEADS = 8
CHUNK = 64
LEFT_CHUNKS = 8
REL_CLIP = 128
MEM_HEADS = 4
EPS = 1e-6
NEG_INF = -1e30

LANES = 128
HEAD_PAD = 128
V_ROWS = 80
Q_ROWS = 80
ONES_LANE = LANES - 1
N_SPLIT = 3
TOKEN_TILE = 512
FOX_BLOCK = 1024
FOX_CHUNKS = 4
FOX_GROUP = 4
PV_LAG = 2
MIX_GROUPS = 2
LOG2E = 1.4426950408889634
SKIP_LOG2 = 150.0
NORM_MARGIN = 1.02
BOUND_LIMIT = 100.0
VMEM_LIMIT = 56 * 1024 * 1024


def _params(semantics):
    return pltpu.CompilerParams(dimension_semantics=semantics, vmem_limit_bytes=VMEM_LIMIT)


def _const_spec(shape):
    nd = len(shape)
    return pl.BlockSpec(shape, lambda *_: (0,) * nd, pipeline_mode=pl.Buffered(1))


def _rms_rows(x, g):
    ms = jnp.mean(x * x, axis=-1, keepdims=True)
    return x * lax.rsqrt(ms + EPS) * g


def _split3_lanes(v):
    lane = lax.broadcasted_iota(jnp.int32, v.shape, 1)
    t1 = v.astype(BF16)
    r1 = v - t1.astype(F32)
    t2 = r1.astype(BF16)
    r2 = r1 - t2.astype(F32)
    t3 = r2.astype(BF16)
    return jnp.where(lane < FOX_HEADS, t1, jnp.where(lane < 2 * FOX_HEADS, t2, t3))


def _dot(a, b):
    return jnp.dot(a, b, preferred_element_type=F32)


def _dot_nt(a, b):
    return lax.dot_general(a, b, (((1,), (1,)), ((), ())), preferred_element_type=F32)


def _inproj_kernel(x_ref, g_ref, wf_ref, bf_ref, wkf_ref, wqf_ref, wvf_ref, wkc_ref, wqc_ref, wvc_ref,
                   ek_ref, eq_ref, ones_ref, gk_ref, gq_ref,
                   kf_ref, qf_ref, vf_ref, kc_ref, qc_ref, vc_ref, ks_ref, qs_ref, qn_ref, cr_ref, carry_ref):
    tm = x_ref.shape[1]

    @pl.when(pl.program_id(1) == 0)
    def _():
        carry_ref[...] = jnp.zeros_like(carry_ref)

    h = _rms_rows(x_ref[0], g_ref[...]).astype(BF16)

    z = _dot(h, wf_ref[...]) + bf_ref[...]

    ones_rows = jnp.tile(ones_ref[...], (1, tm // LANES))
    vf_ref[0, 0] = (_dot_nt(wvf_ref[...], h) + ones_rows).astype(BF16)
    kc_ref[0, 0] = _dot(h, wkc_ref[...]).astype(BF16)
    qc_ref[0] = _dot_nt(wqc_ref[...], h).astype(BF16)
    vc_ref[0, 0] = (_dot_nt(wvc_ref[...], h) + ones_rows).astype(BF16)
    k_pairs = _dot(h, wkf_ref[...]).astype(BF16).astype(F32)
    q_rows = _dot_nt(wqf_ref[...], h)

    lf = jnp.minimum(z, 0.0) - jnp.log1p(jnp.exp(-jnp.abs(z)))
    row = lax.broadcasted_iota(jnp.int32, (tm, tm), 0)
    col = lax.broadcasted_iota(jnp.int32, (tm, tm), 1)
    tri = (col <= row).astype(BF16)
    l1 = lf.astype(BF16)
    l2 = (lf - l1.astype(F32)).astype(BF16)
    c = _dot(tri, l1) + _dot(tri, l2) + carry_ref[...]
    carry_ref[...] = c[tm - 1:tm, :]

    lane = lax.broadcasted_iota(jnp.int32, c.shape, 1)
    gate = jnp.where(lane == ONES_LANE, jnp.ones_like(c).astype(BF16), _split3_lanes(c * LOG2E))

    k_gate = _dot(gate, ek_ref[...])
    feature_lane = lax.broadcasted_iota(jnp.int32, (tm, HEAD_PAD), 1) < HEAD_DIM
    k_heads = []
    for hh in range(FOX_HEADS):
        group = slice((hh // 2) * HEAD_PAD, (hh // 2 + 1) * HEAD_PAD)
        pair, pair_gate = k_pairs[:, group], k_gate[:, group]
        if hh % 2:
            pair, pair_gate = pltpu.roll(pair, HEAD_DIM, 1), pltpu.roll(pair_gate, HEAD_DIM, 1)
        k_heads.append(jnp.where(feature_lane, pair, pair_gate))
    kf_ref[0, 0] = jnp.concatenate(k_heads, axis=1).astype(BF16)
    qf = (q_rows + _dot_nt(eq_ref[...], gate)).astype(BF16)
    qf_ref[0] = qf

    qf32 = qf.astype(F32)
    k_norm2 = jnp.max(_dot((k_pairs * k_pairs).astype(BF16), gk_ref[...]), axis=0, keepdims=True)
    q_norm2_rows = _dot(gq_ref[...], (qf32 * qf32).astype(BF16))
    q_norm2 = jnp.max(q_norm2_rows, axis=1, keepdims=True)
    c2 = c * LOG2E
    srow = lax.broadcasted_iota(jnp.int32, ks_ref.shape[2:], 0)
    ks_ref[0, 0] = jnp.where(srow == 0, c2[tm - 1:tm, :], jnp.where(srow == 1, c2[0:1, :], k_norm2))
    qs_ref[0, 0] = jnp.broadcast_to(q_norm2, qs_ref.shape[2:])
    qn_ref[0] = q_norm2_rows
    cr_ref[0] = jnp.transpose(c2)[:qn_ref.shape[1], :]


def _inproj(x, g, wp):
    B, S, D = x.shape
    tm = TOKEN_TILE
    nk = S // tm
    hp = FOX_HEADS * HEAD_PAD
    hc = CHUNK_HEADS * HEAD_DIM
    hv = FOX_HEADS * V_ROWS
    tile = lambda b, i: (b, i, 0)
    out_shapes = (
        jax.ShapeDtypeStruct((B, nk, tm, hp), BF16),
        jax.ShapeDtypeStruct((B, FOX_HEADS * Q_ROWS, S), BF16),
        jax.ShapeDtypeStruct((B, nk, hv, tm), BF16),
        jax.ShapeDtypeStruct((B, nk, tm, hc), BF16),
        jax.ShapeDtypeStruct((B, hc, S), BF16),
        jax.ShapeDtypeStruct((B, nk, hv, tm), BF16),
        jax.ShapeDtypeStruct((B, nk, 8, LANES), F32),
        jax.ShapeDtypeStruct((B, nk, 2 * FOX_HEADS, LANES), F32),
        jax.ShapeDtypeStruct((B, 2 * FOX_HEADS, S), F32),
        jax.ShapeDtypeStruct((B, 2 * FOX_HEADS, S), F32),
    )
    stat_spec = lambda r: pl.BlockSpec((1, 1, r, LANES), lambda b, i: (b, i, 0, 0))
    k_spec = lambda w: pl.BlockSpec((1, 1, tm, w), lambda b, i: (b, i, 0, 0))
    q_spec = lambda w: pl.BlockSpec((1, w, tm), lambda b, i: (b, 0, i))
    v_spec = pl.BlockSpec((1, 1, hv, tm), lambda b, i: (b, i, 0, 0))
    weights = [g, wp["wf"], wp["bf"], wp["wkf"], wp["wqf"], wp["wvf"], wp["wkc"], wp["wqc"], wp["wvc"],
               wp["ek"], wp["eq"], wp["ones_rows"], wp["gk"], wp["gq"]]
    return pl.pallas_call(
        _inproj_kernel,
        grid=(B, nk),
        in_specs=[pl.BlockSpec((1, tm, D), tile)] + [_const_spec(w.shape) for w in weights],
        out_specs=(k_spec(hp), q_spec(FOX_HEADS * Q_ROWS), v_spec, k_spec(hc), q_spec(hc), v_spec,
                   stat_spec(8), stat_spec(2 * FOX_HEADS), q_spec(2 * FOX_HEADS), q_spec(2 * FOX_HEADS)),
        out_shape=out_shapes,
        scratch_shapes=[pltpu.VMEM((1, LANES), F32)],
        compiler_params=_params(("arbitrary", "arbitrary")),
        name="inproj",
    )(x, *weights)


def _fox_kernel(thr_ref, ck_ref, kn_ref, slow_ref, q_ref, qnext_ref, qn2_ref, crow_ref, k_ref, v_ref, o_ref,
                s0_ref, s1_ref, mx0_ref, mx1_ref, sf_ref, mxf_ref, acc_ref, m_ref):
    iq = pl.program_id(2)
    nq = pl.num_programs(2)
    tq = q_ref.shape[2]
    nkb, tk = k_ref.shape[1:3]
    tv = v_ref.shape[3]
    def contraction_rows(q):
        return jnp.concatenate([q, jnp.zeros((k_ref.shape[3] - q.shape[0], tq), q.dtype)], axis=0)

    qT = contraction_rows(q_ref[0])
    m_ref[...] = jnp.full(m_ref.shape, NEG_INF, F32)
    acc_ref[...] = jnp.zeros_like(acc_ref)
    cur = lax.bitwise_and(iq, 1)
    nxt = 1 - cur
    head = pl.program_id(0) * pl.num_programs(1) + pl.program_id(1)

    def first_block(i):
        thr = thr_ref[head * nq + i]
        count = jnp.int32(0)
        for j in range(nkb):
            skip = jnp.logical_and(j < i, thr < ck_ref[head * nkb + j])
            count = count + jnp.where(skip, 1, 0)
        return count

    j0 = first_block(iq)

    ckeys = tk // FOX_CHUNKS
    chunk_rows = [slice(c * ckeys, (c + 1) * ckeys) for c in range(FOX_CHUNKS)]

    def causal(c, s):
        key = lax.broadcasted_iota(jnp.int32, (ckeys, tq), 0) + c * ckeys
        qry = lax.broadcasted_iota(jnp.int32, (ckeys, tq), 1)
        return jnp.where(key <= qry, s, NEG_INF)

    def diagonal_max(s_ref):
        return functools.reduce(jnp.maximum, [jnp.max(causal(c, s_ref[rows, :]), axis=0, keepdims=True)
                                              for c, rows in enumerate(chunk_rows)])

    def step(jn, q_n, sn_ref, mxn_ref, jc=None, sc_ref=None, mx_c=None, diagonal=False):
        if jc is not None:
            m_old = m_ref[...]
            m_new = jnp.maximum(m_old, mx_c)
        mx_parts, pv = [], None
        for c, rows in enumerate(chunk_rows):
            if jc is not None:
                s_c = sc_ref[rows, :]
                p = jnp.exp2((causal(c, s_c) if diagonal else s_c) - m_new).astype(BF16)
            s = _dot(k_ref[0, jn, rows, :], q_n)
            sn_ref[rows, :] = s
            mx_parts.append(jnp.max(s, axis=0, keepdims=True))
            if jc is not None:
                off = (c * ckeys) % tv
                part = _dot(v_ref[0, jc * (tk // tv) + (c * ckeys) // tv, :, off:off + ckeys], p)
                pv = part if pv is None else pv + part
        mxn_ref[...] = functools.reduce(jnp.maximum, mx_parts)
        if jc is not None:
            acc_ref[...] = jnp.exp2(m_old - m_new) * acc_ref[...] + pv
            m_ref[...] = m_new

    def diagonal_and_handover(s_ref):
        step(first_block(jnp.minimum(iq + 1, nq - 1)), contraction_rows(qnext_ref[0]), sf_ref.at[nxt], mxf_ref.at[nxt],
             iq, s_ref, diagonal_max(s_ref), diagonal=True)

    def exact_max_path():
        @pl.when(iq == 0)
        def _():
            step(0, qT, sf_ref.at[cur], mxf_ref.at[cur])

        @pl.when(j0 == iq)
        def _():
            diagonal_and_handover(sf_ref.at[cur])

        @pl.when(j0 < iq)
        def _():
            step(j0 + 1, qT, s1_ref, mx1_ref, j0, sf_ref.at[cur], mxf_ref[cur])
            rest = iq - j0 - 1

            def pair(t, carry):
                j = j0 + 2 * t + 1
                step(j + 1, qT, s0_ref, mx0_ref, j, s1_ref, mx1_ref[...])
                step(j + 2, qT, s1_ref, mx1_ref, j + 1, s0_ref, mx0_ref[...])
                return carry

            lax.fori_loop(0, lax.shift_right_logical(rest, 1), pair, 0)
            odd = lax.bitwise_and(rest, 1)

            @pl.when(odd == 0)
            def _():
                diagonal_and_handover(s1_ref)

            @pl.when(odd == 1)
            def _():
                step(iq, qT, s0_ref, mx0_ref, iq - 1, s1_ref, mx1_ref[...])
                diagonal_and_handover(s0_ref)

    def bound_path():
        q_norm = jnp.sqrt(qn2_ref[0, 0]) * NORM_MARGIN
        c_t = crow_ref[0, 0]

        def pv_chunk(j, c, p):
            off = (c * ckeys) % tv
            return _dot(v_ref[0, j * (tk // tv) + (c * ckeys) // tv, :, off:off + ckeys], p)

        def run_blocks(blocks):
            m = m_ref[...]
            stabiliser = []
            for _, bound, _ in blocks:
                m_new = jnp.maximum(m, bound)
                stabiliser.append((m, m_new))
                m = m_new
            m_ref[...] = m
            stream = [(b, c) for b in range(len(blocks)) for c in range(FOX_CHUNKS)]
            weights, pv = {}, [None] * len(blocks)

            def finish(b, c):
                w, q0 = weights.pop((b, c))
                part = pv_chunk(blocks[b][0], c, w)
                if q0:
                    part = jnp.concatenate([jnp.zeros((part.shape[0], q0), F32), part], axis=1)
                pv[b] = part if pv[b] is None else pv[b] + part
                if c == FOX_CHUNKS - 1:
                    m_old, m_new = stabiliser[b]
                    acc_ref[...] = jnp.exp2(m_old - m_new) * acc_ref[...] + pv[b]

            for i, (b, c) in enumerate(stream):
                j, _, diagonal = blocks[b]
                q0 = c * ckeys if diagonal else 0
                s = _dot(k_ref[0, j, chunk_rows[c], :], qT[:, q0:])
                if diagonal:
                    key = lax.broadcasted_iota(jnp.int32, s.shape, 0)
                    qry = lax.broadcasted_iota(jnp.int32, s.shape, 1)
                    s = jnp.where(key <= qry, s, NEG_INF)
                weights[b, c] = jnp.exp2(s - stabiliser[b][1][:, q0:]).astype(BF16), q0
                if i >= PV_LAG:
                    finish(*stream[i - PV_LAG])
            for item in stream[-PV_LAG:]:
                finish(*item)

        def earlier(j):
            return j, q_norm * kn_ref[head * nkb + j] + (c_t - ck_ref[head * nkb + j]), False

        diagonal = (iq, q_norm * kn_ref[head * nkb + iq], True)

        def group(t, carry):
            run_blocks([earlier(j0 + FOX_GROUP * t + b) for b in range(FOX_GROUP)])
            return carry

        n_earlier = iq - j0
        n_groups = lax.shift_right_logical(n_earlier, FOX_GROUP.bit_length() - 1)
        lax.fori_loop(0, n_groups, group, 0)
        left = n_earlier - FOX_GROUP * n_groups

        @pl.when(left >= 2)
        def _():
            run_blocks([earlier(j0 + FOX_GROUP * n_groups), earlier(j0 + FOX_GROUP * n_groups + 1)])

        odd = lax.bitwise_and(left, 1)

        @pl.when(odd == 0)
        def _():
            run_blocks([diagonal])

        @pl.when(odd == 1)
        def _():
            run_blocks([earlier(iq - 1), diagonal])

    use_bound = slow_ref[head] == 0
    pl.when(use_bound)(bound_path)
    pl.when(jnp.logical_not(use_bound))(exact_max_path)

    acc = acc_ref[...]
    o_ref[0] = (acc[:HEAD_DIM] / acc[HEAD_DIM:HEAD_DIM + 1]).astype(o_ref.dtype)


def _fox_skip_tables(kstats, qstats):
    r = FOX_BLOCK // TOKEN_TILE
    B, nt = kstats.shape[:2]
    H = FOX_HEADS
    per_block = lambda a: a.reshape(B, nt // r, r, H)
    ck = per_block(kstats[:, :, 0, :H])[:, :, r - 1]
    cq = per_block(kstats[:, :, 1, :H])[:, :, 0]
    k_norm = jnp.sqrt(jnp.max(kstats[:, :, 2, :H], axis=1))
    q_norm = jnp.sqrt(jnp.max(per_block(qstats[:, :, :H, 0]), axis=2))
    gap = 2.0 * NORM_MARGIN * q_norm * k_norm[:, None, :]
    thr = cq + gap + SKIP_LOG2
    kn = jnp.sqrt(jnp.max(per_block(kstats[:, :, 2, :H]), axis=2)) * NORM_MARGIN
    slow = jnp.logical_not(jnp.max(gap, axis=1) < BOUND_LIMIT).astype(jnp.int32)
    flat = lambda a: jnp.transpose(a, (0, 2, 1)).reshape(-1).astype(F32)
    return flat(thr), flat(ck), flat(kn), slow.reshape(-1)


def _fox(kf, qf, vf, kstats, qstats, q_norm2, c_rows):
    B, nv, tv, hp = kf.shape
    S = qf.shape[2]
    H = hp // HEAD_PAD
    tq = tk = FOX_BLOCK
    kf = kf.reshape(B, S // tk, tk, hp)
    nq = S // tq
    tables = _fox_skip_tables(kstats, qstats)
    per_query = lambda a: a.reshape(B, a.shape[1], 1, S)
    row_spec = pl.BlockSpec((1, 1, 1, tq), lambda b, h, i, *_: (b, h, 0, i))
    grid_spec = pltpu.PrefetchScalarGridSpec(
        num_scalar_prefetch=len(tables),
        grid=(B, H, nq),
        in_specs=[
            pl.BlockSpec((1, Q_ROWS, tq), lambda b, h, i, *_: (b, h, i)),
            pl.BlockSpec((1, Q_ROWS, tq), lambda b, h, i, *_: (b, h, jnp.minimum(i + 1, nq - 1))),
            row_spec,
            row_spec,
            pl.BlockSpec((1, S // tk, tk, HEAD_PAD), lambda b, h, i, *_: (b, 0, 0, h)),
            pl.BlockSpec((1, nv, V_ROWS, tv), lambda b, h, i, *_: (b, 0, h, 0)),
        ],
        out_specs=pl.BlockSpec((1, HEAD_DIM, tq), lambda b, h, i, *_: (b, h, i)),
        scratch_shapes=[pltpu.VMEM((tk, tq), F32), pltpu.VMEM((tk, tq), F32),
                        pltpu.VMEM((1, tq), F32), pltpu.VMEM((1, tq), F32),
                        pltpu.VMEM((2, tk, tq), F32), pltpu.VMEM((2, 1, tq), F32),
                        pltpu.VMEM((V_ROWS, tq), F32), pltpu.VMEM((1, tq), F32)],
    )
    return pl.pallas_call(
        _fox_kernel,
        grid_spec=grid_spec,
        out_shape=jax.ShapeDtypeStruct((B, H * HEAD_DIM, S), BF16),
        compiler_params=_params(("arbitrary", "arbitrary", "arbitrary")),
        name="fox",
    )(*tables, qf, qf, per_query(q_norm2), per_query(c_rows), kf, vf)


BAND_BLOCKS = LEFT_CHUNKS * CHUNK // LANES


def _bias_block_id(delta):
    if delta < 0 or delta > BAND_BLOCKS:
        return None
    return {0: 0, 1: 1, BAND_BLOCKS: 3}.get(delta, 2)


def _chunk_kernel(q_ref, kp_ref, kc_ref, vp_ref, vc_ref, blk_ref, o_ref):
    iq = pl.program_id(1)
    tq = q_ref.shape[2]
    nb = tq // LANES
    before_start = jnp.where(iq > 0, 0.0, NEG_INF).astype(F32)
    zero_blk = jnp.zeros((LANES, LANES), BF16)
    row = lax.broadcasted_iota(jnp.int32, (2 * HEAD_DIM, tq), 0)

    def scores(h):
        pair, odd = divmod(h, 2)
        lanes = slice(pair * 2 * HEAD_DIM, (pair + 1) * 2 * HEAD_DIM)
        q_pair = q_ref[0, lanes, :]
        qT = jnp.where((row >= HEAD_DIM) == bool(odd), q_pair, jnp.zeros_like(q_pair))
        return _dot(kp_ref[0, 0, :, lanes], qT), _dot(kc_ref[0, 0, :, lanes], qT)

    ahead = [scores(0), scores(1)]
    for h in range(CHUNK_HEADS):
        s_half = ahead.pop(0)
        if h + 2 < CHUNK_HEADS:
            ahead.append(scores(h + 2))
        p_blocks = {}
        for b in range(nb):
            cols = slice(b * LANES, (b + 1) * LANES)
            entries = []
            for half in (0, 1):
                for al in range(nb):
                    bid = _bias_block_id(b - (al - nb * (1 - half)))
                    if bid is None:
                        continue
                    sb = s_half[half][al * LANES:(al + 1) * LANES, cols] + blk_ref[h, bid]
                    if half == 0:
                        sb = sb + before_start
                    entries.append((half, al, sb))
            m = functools.reduce(jnp.maximum, [jnp.max(sb, axis=0, keepdims=True) for _, _, sb in entries])
            for half, al, sb in entries:
                p_blocks[half, al, b] = jnp.exp2(sb - m).astype(BF16)
        p_half = [jnp.concatenate([jnp.concatenate([p_blocks.get((half, al, b), zero_blk) for b in range(nb)],
                                                   axis=1) for al in range(nb)], axis=0) for half in (0, 1)]
        vrows = slice(h * V_ROWS, (h + 1) * V_ROWS)
        acc = _dot(vc_ref[0, 0, vrows, :], p_half[1]) + _dot(vp_ref[0, 0, vrows, :], p_half[0])
        o_ref[0, h * HEAD_DIM:(h + 1) * HEAD_DIM, :] = (
            acc[:HEAD_DIM] / acc[HEAD_DIM:HEAD_DIM + 1]).astype(o_ref.dtype)


def _chunk(kc, qc, vc, blocks):
    B, nk, tk, hc = kc.shape
    S = qc.shape[2]
    hv = vc.shape[2]
    tq = tk
    assert tq == LEFT_CHUNKS * CHUNK and 2 * CHUNK == LANES
    prev = lambda i: jnp.maximum(i - 1, 0)
    return pl.pallas_call(
        _chunk_kernel,
        grid=(B, S // tq),
        in_specs=[
            pl.BlockSpec((1, hc, tq), lambda b, i: (b, 0, i)),
            pl.BlockSpec((1, 1, tk, hc), lambda b, i: (b, prev(i), 0, 0)),
            pl.BlockSpec((1, 1, tk, hc), lambda b, i: (b, i, 0, 0)),
            pl.BlockSpec((1, 1, hv, tk), lambda b, i: (b, prev(i), 0, 0)),
            pl.BlockSpec((1, 1, hv, tk), lambda b, i: (b, i, 0, 0)),
            _const_spec(blocks.shape),
        ],
        out_specs=pl.BlockSpec((1, hc, tq), lambda b, i: (b, 0, i)),
        out_shape=jax.ShapeDtypeStruct((B, hc, S), BF16),
        compiler_params=_params(("arbitrary", "arbitrary")),
        name="chunk",
    )(qc, kc, kc, vc, vc, blocks)


def _chunk_bias_blocks(rel_bias):
    heads = rel_bias.shape[0]
    n = LANES
    span = 3 * n - 1
    dist = np.arange(span) - (n - 1)
    by_dist = rel_bias.astype(F32)[:, np.clip(dist, -(CHUNK - 1), REL_CLIP) + (CHUNK - 1)] * LOG2E
    padded = jnp.pad(by_dist, ((0, 0), (0, 1)))
    skew = jnp.broadcast_to(padded[:, None, :], (heads, n, span + 1)).reshape(heads, n * (span + 1))
    near = skew[:, :n * span].reshape(heads, n, span)[:, :, n - 1:]
    key_chunk = (np.arange(n) // CHUNK)[:, None]
    qry_chunk = (np.arange(n) // CHUNK)[None, :]
    clipped = jnp.broadcast_to(by_dist[:, -1][:, None, None], (heads, n, n))
    blocks = [
        jnp.where((qry_chunk >= key_chunk)[None], near[:, :, :n], NEG_INF),
        near[:, :, n:],
        clipped,
        jnp.where((2 * BAND_BLOCKS + qry_chunk - key_chunk <= LEFT_CHUNKS)[None], clipped, NEG_INF),
    ]
    return jnp.stack(blocks, axis=1)


def _memkv_kernel(mem_ref, g_ref, w_ref, kv_ref):
    m = _rms_rows(mem_ref[0], g_ref[...]).astype(BF16)
    kv_ref[0] = _dot(m, w_ref[...]).astype(BF16)


def _memkv(mem, g, w):
    B, N, D = mem.shape
    return pl.pallas_call(
        _memkv_kernel,
        grid=(B,),
        in_specs=[pl.BlockSpec((1, N, D), lambda b: (b, 0, 0)), _const_spec(g.shape), _const_spec(w.shape)],
        out_specs=pl.BlockSpec((1, N, 2 * D), lambda b: (b, 0, 0)),
        out_shape=jax.ShapeDtypeStruct((B, N, 2 * D), BF16),
        compiler_params=_params(("arbitrary",)),
        name="memkv",
    )(mem, g, w)


def _mix_mem_kernel(x_ref, of_ref, oc_ref, kv_ref, gfo_ref, gco_ref, wout_ref, gmp_ref,
                    gmq_ref, wq_ref, wo_ref, gmo_ref, y_ref):
    tm, d = x_ref.shape[1:]
    dh = d // MEM_HEADS
    kv = kv_ref[0]
    groups = [slice(r * tm // MIX_GROUPS, (r + 1) * tm // MIX_GROUPS) for r in range(MIX_GROUPS)]

    def mixer_input(rows):
        of = jnp.transpose(of_ref[0, :, rows].astype(F32))
        oc = jnp.transpose(oc_ref[0, :, rows].astype(F32))
        return jnp.concatenate([_rms_rows(of, gfo_ref[...]), _rms_rows(oc, gco_ref[...])], axis=-1).astype(BF16)

    def cross_attention(q):
        outs = []
        for hh in range(MEM_HEADS):
            sl = slice(hh * dh, (hh + 1) * dh)
            s = _dot_nt(q[:, sl], kv[:, sl])
            p = jnp.exp2(s - jnp.max(s, axis=-1, keepdims=True))
            l = jnp.sum(p, axis=-1, keepdims=True)
            outs.append(_dot(p.astype(BF16), kv[:, d + hh * dh:d + (hh + 1) * dh]) / l)
        return jnp.concatenate(outs, axis=-1).astype(BF16)

    cat = [mixer_input(rows) for rows in groups]
    x1 = [x_ref[0, rows, :] + _rms_rows(_dot(c, wout_ref[...]), gmp_ref[...]) for rows, c in zip(groups, cat)]
    q = [_dot(_rms_rows(v, gmq_ref[...]).astype(BF16), wq_ref[...]).astype(BF16) for v in x1]
    ca = [cross_attention(v) for v in q]
    for rows, v, c in zip(groups, x1, ca):
        y_ref[0, rows, :] = v + _rms_rows(_dot(c, wo_ref[...]), gmo_ref[...])


def _mix_mem(x, of, oc, kv, wp):
    B, S, D = x.shape
    tm = MIX_GROUPS * TOKEN_TILE
    half = of.shape[1]
    weights = [wp["g_fox_out"], wp["g_chunk_out"], wp["w_out"], wp["g_mix_post"],
               wp["g_mem_pre"], wp["w_mem_q"], wp["w_mem_o"], wp["g_mem_post"]]
    return pl.pallas_call(
        _mix_mem_kernel,
        grid=(B, S // tm),
        in_specs=[
            pl.BlockSpec((1, tm, D), lambda b, i: (b, i, 0)),
            pl.BlockSpec((1, half, tm), lambda b, i: (b, 0, i)),
            pl.BlockSpec((1, half, tm), lambda b, i: (b, 0, i)),
            pl.BlockSpec((1,) + kv.shape[1:], lambda b, i: (b, 0, 0)),
        ] + [_const_spec(w.shape) for w in weights],
        out_specs=pl.BlockSpec((1, tm, D), lambda b, i: (b, i, 0)),
        out_shape=jax.ShapeDtypeStruct((B, S, D), F32),
        compiler_params=_params(("arbitrary", "arbitrary")),
        name="mix_mem",
    )(x, of, oc, kv, *weights)


FFN_CHUNK = 1024


def _ffn_kernel(x_ref, gpre_ref, wg_ref, wu_ref, wd_ref, gpost_ref, y_ref, act_ref):
    x = x_ref[0]
    h = _rms_rows(x, gpre_ref[...]).astype(BF16)
    dff = wg_ref.shape[1]
    for c0 in range(0, dff, FFN_CHUNK):
        c1 = min(c0 + FFN_CHUNK, dff)
        g = _dot(h, wg_ref[:, c0:c1])
        u = _dot(h, wu_ref[:, c0:c1])
        act_ref[:, c0:c1] = (jax.nn.silu(g) * u).astype(BF16)
    y_ref[0] = x + _rms_rows(_dot(act_ref[...], wd_ref[...]), gpost_ref[...])


def _ffn(x, wp):
    B, S, D = x.shape
    tm = TOKEN_TILE
    weights = [wp["g_ffn_pre"], wp["w_gate"], wp["w_up"], wp["w_down"], wp["g_ffn_post"]]
    dff = wp["w_gate"].shape[1]
    return pl.pallas_call(
        _ffn_kernel,
        grid=(B, S // tm),
        in_specs=[pl.BlockSpec((1, tm, D), lambda b, i: (b, i, 0))] + [_const_spec(w.shape) for w in weights],
        out_specs=pl.BlockSpec((1, tm, D), lambda b, i: (b, i, 0)),
        out_shape=jax.ShapeDtypeStruct((B, S, D), F32),
        scratch_shapes=[pltpu.VMEM((tm, dff), BF16)],
        compiler_params=_params(("arbitrary", "arbitrary")),
        name="ffn",
    )(x, *weights)


def _pad_heads_cols(w, heads, width):
    d = w.shape[0]
    w = w.reshape(d, heads, HEAD_DIM)
    w = jnp.pad(w, ((0, 0), (0, 0), (0, width - HEAD_DIM)))
    return w.reshape(d, heads * width)


def _placement(entries, rows, cols):
    m = np.zeros((rows, cols), np.float32)
    for r, c, v in entries:
        m[r, c] = v
    return jnp.asarray(m, BF16)


def _prep_layer(l, g_mix_pre, w_in, b_fgate, g_fox_out, g_chunk_out, w_out, g_mix_post, g_mem_pre, g_mem_kv,
                w_mem_q, w_mem_kv, w_mem_o, g_mem_post, g_ffn_pre, w_gate_up, w_down, g_ffn_post):
    d = w_in.shape[1]
    fw = FOX_HEADS * HEAD_DIM
    cw = CHUNK_HEADS * HEAD_DIM
    w = w_in[l]
    o = 0
    wq_f, wk_f, wv_f = w[:, o:o + fw], w[:, o + fw:o + 2 * fw], w[:, o + 2 * fw:o + 3 * fw]
    o += 3 * fw
    w_f = w[:, o:o + FOX_HEADS]
    o += FOX_HEADS
    wq_c, wk_c, wv_c = w[:, o:o + cw], w[:, o + cw:o + 2 * cw], w[:, o + 2 * cw:o + 3 * cw]
    scale = HEAD_DIM ** -0.5 * LOG2E

    row = lambda v: v.astype(F32)[None, :]
    wf3 = jnp.pad(jnp.tile(w_f, (1, N_SPLIT)), ((0, 0), (0, LANES - N_SPLIT * FOX_HEADS)))
    bf3 = jnp.pad(jnp.tile(b_fgate[l].astype(F32), N_SPLIT), (0, LANES - N_SPLIT * FOX_HEADS))[None, :]

    ek, eq = [], []
    for h in range(FOX_HEADS):
        for i in range(N_SPLIT):
            gate_lane = (h // 2) * HEAD_PAD + (HEAD_DIM if h % 2 == 0 else 0)
            ek.append((ONES_LANE, gate_lane + i, 1.0))
            ek.append((i * FOX_HEADS + h, gate_lane + N_SPLIT + i, -1.0))
            eq.append((h * Q_ROWS + HEAD_DIM + i, i * FOX_HEADS + h, 1.0))
            eq.append((h * Q_ROWS + HEAD_DIM + N_SPLIT + i, ONES_LANE, 1.0))
    gk = [(h * HEAD_DIM + d, h, 1.0) for h in range(FOX_HEADS) for d in range(HEAD_DIM)]
    gq = [(h, h * Q_ROWS + d, 1.0) for h in range(FOX_HEADS) for d in range(HEAD_DIM)]
    ones_rows = np.zeros((FOX_HEADS * V_ROWS, LANES), np.float32)
    ones_rows[np.arange(FOX_HEADS) * V_ROWS + HEAD_DIM] = 1.0

    dff = w_gate_up.shape[2] // 2
    mem_scale = (w_mem_q.shape[2] // MEM_HEADS) ** -0.5 * LOG2E
    return dict(
        g_mix_pre=row(g_mix_pre[l]),
        wf=wf3.astype(BF16), bf=bf3,
        wkf=wk_f.astype(BF16),
        wqf=_pad_heads_cols(wq_f * scale, FOX_HEADS, Q_ROWS).T.astype(BF16),
        wvf=_pad_heads_cols(wv_f, FOX_HEADS, V_ROWS).T.astype(BF16),
        wkc=wk_c.astype(BF16),
        wqc=(wq_c * scale).T.astype(BF16),
        wvc=_pad_heads_cols(wv_c, CHUNK_HEADS, V_ROWS).T.astype(BF16),
        ek=_placement(ek, LANES, FOX_HEADS * HEAD_DIM),
        eq=_placement(eq, FOX_HEADS * Q_ROWS, LANES),
        ones_rows=jnp.asarray(ones_rows),
        gk=_placement(gk, FOX_HEADS * HEAD_DIM, LANES),
        gq=_placement(gq, 2 * FOX_HEADS, FOX_HEADS * Q_ROWS),
        g_fox_out=row(g_fox_out[l]), g_chunk_out=row(g_chunk_out[l]),
        w_out=w_out[l].astype(BF16), g_mix_post=row(g_mix_post[l]),
        g_mem_pre=row(g_mem_pre[l]), g_mem_kv=row(g_mem_kv[l]),
        w_mem_q=(w_mem_q[l] * mem_scale).astype(BF16), w_mem_kv=w_mem_kv[l].astype(BF16),
        w_mem_o=w_mem_o[l].astype(BF16), g_mem_post=row(g_mem_post[l]),
        g_ffn_pre=row(g_ffn_pre[l]),
        w_gate=w_gate_up[l][:, :dff].astype(BF16), w_up=w_gate_up[l][:, dff:].astype(BF16),
        w_down=w_down[l].astype(BF16), g_ffn_post=row(g_ffn_post[l]),
    )


def kernel(x, mem, g_mix_pre, w_in, b_fgate, rel_bias, g_fox_out, g_chunk_out, w_out, g_mix_post, g_mem_pre,
           g_mem_kv, w_mem_q, w_mem_kv, w_mem_o, g_mem_post, g_ffn_pre, w_gate_up, w_down, g_ffn_post):
    depth = w_in.shape[0]
    for l in range(depth):
        wp = _prep_layer(l, g_mix_pre, w_in, b_fgate, g_fox_out, g_chunk_out, w_out, g_mix_post, g_mem_pre,
                         g_mem_kv, w_mem_q, w_mem_kv, w_mem_o, g_mem_post, g_ffn_pre, w_gate_up, w_down,
                         g_ffn_post)
        kf, qf, vf, kc, qc, vc, kstats, qstats, q_norm2, c_rows = _inproj(x, wp["g_mix_pre"], wp)
        o_f = _fox(kf, qf, vf, kstats, qstats, q_norm2, c_rows)
        o_c = _chunk(kc, qc, vc, _chunk_bias_blocks(rel_bias[l]))
        kv = _memkv(mem, wp["g_mem_kv"], wp["w_mem_kv"])
        x = _mix_mem(x, o_f, o_c, kv, wp)
        x = _ffn(x, wp)
    return x
```

```python
import functools

import jax
import jax.numpy as jnp
import numpy as np
from jax import lax
from jax.experimental import pallas as pl
from jax.experimental.pallas import tpu as pltpu

F32 = jnp.float32
BF16 = jnp.bfloat16

HEAD_DIM = 64
FOX_HEADS = 8
CHUNK_HEADS = 8
CHUNK = 64
LEFT_CHUNKS = 8
REL_CLIP = 128
MEM_HEADS = 4
EPS = 1e-6
NEG_INF = -1e30

LANES = 128
HEAD_PAD = 128
V_ROWS = 80
Q_ROWS = 80
ONES_LANE = LANES - 1
N_SPLIT = 3
TOKEN_TILE = 512
FOX_BLOCK = 1024
FOX_CHUNKS = 4
FOX_GROUP = 4
PV_LAG = 2
MIX_GROUPS = 2
LOG2E = 1.4426950408889634
SKIP_LOG2 = 150.0
NORM_MARGIN = 1.02
BOUND_LIMIT = 100.0
VMEM_LIMIT = 56 * 1024 * 1024


def _params(semantics):
    return pltpu.CompilerParams(dimension_semantics=semantics, vmem_limit_bytes=VMEM_LIMIT)


def _const_spec(shape):
    nd = len(shape)
    return pl.BlockSpec(shape, lambda *_: (0,) * nd, pipeline_mode=pl.Buffered(1))


def _rms_rows(x, g):
    ms = jnp.mean(x * x, axis=-1, keepdims=True)
    return x * lax.rsqrt(ms + EPS) * g


def _split3_lanes(v):
    lane = lax.broadcasted_iota(jnp.int32, v.shape, 1)
    t1 = v.astype(BF16)
    r1 = v - t1.astype(F32)
    t2 = r1.astype(BF16)
    r2 = r1 - t2.astype(F32)
    t3 = r2.astype(BF16)
    return jnp.where(lane < FOX_HEADS, t1, jnp.where(lane < 2 * FOX_HEADS, t2, t3))


def _dot(a, b):
    return jnp.dot(a, b, preferred_element_type=F32)


def _dot_nt(a, b):
    return lax.dot_general(a, b, (((1,), (1,)), ((), ())), preferred_element_type=F32)


def _inproj_kernel(x_ref, g_ref, wf_ref, bf_ref, wkf_ref, wqf_ref, wvf_ref, wkc_ref, wqc_ref, wvc_ref,
                   ek_ref, eq_ref, ones_ref, gk_ref, gq_ref,
                   kf_ref, qf_ref, vf_ref, kc_ref, qc_ref, vc_ref, ks_ref, qs_ref, qn_ref, cr_ref, carry_ref):
    tm = x_ref.shape[1]

    @pl.when(pl.program_id(1) == 0)
    def _():
        carry_ref[...] = jnp.zeros_like(carry_ref)

    h = _rms_rows(x_ref[0], g_ref[...]).astype(BF16)

    z = _dot(h, wf_ref[...]) + bf_ref[...]

    ones_rows = jnp.tile(ones_ref[...], (1, tm // LANES))
    vf = (_dot_nt(wvf_ref[...], h) + ones_rows).astype(BF16)
    for hh in range(FOX_HEADS):
        vf_ref[0, hh, 0] = vf[hh * V_ROWS:(hh + 1) * V_ROWS]
    kc_ref[0, 0] = _dot(h, wkc_ref[...]).astype(BF16)
    qc_ref[0] = _dot_nt(wqc_ref[...], h).astype(BF16)
    vc_ref[0, 0] = (_dot_nt(wvc_ref[...], h) + ones_rows).astype(BF16)
    k_pairs = _dot(h, wkf_ref[...]).astype(BF16).astype(F32)
    q_rows = _dot_nt(wqf_ref[...], h)

    lf = jnp.minimum(z, 0.0) - jnp.log1p(jnp.exp(-jnp.abs(z)))
    row = lax.broadcasted_iota(jnp.int32, (tm, tm), 0)
    col = lax.broadcasted_iota(jnp.int32, (tm, tm), 1)
    tri = (col <= row).astype(BF16)
    l1 = lf.astype(BF16)
    l2 = (lf - l1.astype(F32)).astype(BF16)
    c = _dot(tri, l1) + _dot(tri, l2) + carry_ref[...]
    carry_ref[...] = c[tm - 1:tm, :]

    lane = lax.broadcasted_iota(jnp.int32, c.shape, 1)
    gate = jnp.where(lane == ONES_LANE, jnp.ones_like(c).astype(BF16), _split3_lanes(c * LOG2E))

    k_gate = _dot(gate, ek_ref[...])
    feature_lane = lax.broadcasted_iota(jnp.int32, (tm, HEAD_PAD), 1) < HEAD_DIM
    k_heads = []
    for hh in range(FOX_HEADS):
        group = slice((hh // 2) * HEAD_PAD, (hh // 2 + 1) * HEAD_PAD)
        pair, pair_gate = k_pairs[:, group], k_gate[:, group]
        if hh % 2:
            pair, pair_gate = pltpu.roll(pair, HEAD_DIM, 1), pltpu.roll(pair_gate, HEAD_DIM, 1)
        k_heads.append(jnp.where(feature_lane, pair, pair_gate))
    for hh in range(FOX_HEADS):
        kf_ref[0, hh, 0] = k_heads[hh].astype(BF16)
    qf = (q_rows + _dot_nt(eq_ref[...], gate)).astype(BF16)
    qf_ref[0] = qf

    qf32 = qf.astype(F32)
    k_norm2 = jnp.max(_dot((k_pairs * k_pairs).astype(BF16), gk_ref[...]), axis=0, keepdims=True)
    q_norm2_rows = _dot(gq_ref[...], (qf32 * qf32).astype(BF16))
    q_norm2 = jnp.max(q_norm2_rows, axis=1, keepdims=True)
    c2 = c * LOG2E
    srow = lax.broadcasted_iota(jnp.int32, ks_ref.shape[2:], 0)
    ks_ref[0, 0] = jnp.where(srow == 0, c2[tm - 1:tm, :], jnp.where(srow == 1, c2[0:1, :], k_norm2))
    qs_ref[0, 0] = jnp.broadcast_to(q_norm2, qs_ref.shape[2:])
    qn_ref[0] = q_norm2_rows
    cr_ref[0] = jnp.transpose(c2)[:qn_ref.shape[1], :]


def _inproj(x, g, wp):
    B, S, D = x.shape
    tm = TOKEN_TILE
    nk = S // tm
    hp = FOX_HEADS * HEAD_PAD
    hc = CHUNK_HEADS * HEAD_DIM
    hv = FOX_HEADS * V_ROWS
    tile = lambda b, i: (b, i, 0)
    out_shapes = (
        jax.ShapeDtypeStruct((B, FOX_HEADS, nk, tm, HEAD_PAD), BF16),
        jax.ShapeDtypeStruct((B, FOX_HEADS * Q_ROWS, S), BF16),
        jax.ShapeDtypeStruct((B, FOX_HEADS, nk, V_ROWS, tm), BF16),
        jax.ShapeDtypeStruct((B, nk, tm, hc), BF16),
        jax.ShapeDtypeStruct((B, hc, S), BF16),
        jax.ShapeDtypeStruct((B, nk, hv, tm), BF16),
        jax.ShapeDtypeStruct((B, nk, 8, LANES), F32),
        jax.ShapeDtypeStruct((B, nk, 2 * FOX_HEADS, LANES), F32),
        jax.ShapeDtypeStruct((B, 2 * FOX_HEADS, S), F32),
        jax.ShapeDtypeStruct((B, 2 * FOX_HEADS, S), F32),
    )
    stat_spec = lambda r: pl.BlockSpec((1, 1, r, LANES), lambda b, i: (b, i, 0, 0))
    k_spec = lambda w: pl.BlockSpec((1, 1, tm, w), lambda b, i: (b, i, 0, 0))
    q_spec = lambda w: pl.BlockSpec((1, w, tm), lambda b, i: (b, 0, i))
    v_spec = pl.BlockSpec((1, 1, hv, tm), lambda b, i: (b, i, 0, 0))
    head_major = lambda r, c: pl.BlockSpec((1, FOX_HEADS, 1, r, c), lambda b, i: (b, 0, i, 0, 0))
    weights = [g, wp["wf"], wp["bf"], wp["wkf"], wp["wqf"], wp["wvf"], wp["wkc"], wp["wqc"], wp["wvc"],
               wp["ek"], wp["eq"], wp["ones_rows"], wp["gk"], wp["gq"]]
    return pl.pallas_call(
        _inproj_kernel,
        grid=(B, nk),
        in_specs=[pl.BlockSpec((1, tm, D), tile)] + [_const_spec(w.shape) for w in weights],
        out_specs=(head_major(tm, HEAD_PAD), q_spec(FOX_HEADS * Q_ROWS), head_major(V_ROWS, tm),
                   k_spec(hc), q_spec(hc), v_spec,
                   stat_spec(8), stat_spec(2 * FOX_HEADS), q_spec(2 * FOX_HEADS), q_spec(2 * FOX_HEADS)),
        out_shape=out_shapes,
        scratch_shapes=[pltpu.VMEM((1, LANES), F32)],
        compiler_params=_params(("arbitrary", "arbitrary")),
        name="inproj",
    )(x, *weights)


def _fox_kernel(thr_ref, ck_ref, kn_ref, slow_ref, q_ref, qnext_ref, qn2_ref, crow_ref, k_ref, v_ref, o_ref,
                s0_ref, s1_ref, mx0_ref, mx1_ref, sf_ref, mxf_ref, acc_ref, m_ref):
    iq = pl.program_id(2)
    nq = pl.num_programs(2)
    tq = q_ref.shape[2]
    nkb, tk = k_ref.shape[1:3]
    tv = v_ref.shape[3]
    def contraction_rows(q):
        return jnp.concatenate([q, jnp.zeros((k_ref.shape[3] - q.shape[0], tq), q.dtype)], axis=0)

    qT = contraction_rows(q_ref[0])
    m_ref[...] = jnp.full(m_ref.shape, NEG_INF, F32)
    acc_ref[...] = jnp.zeros_like(acc_ref)
    cur = lax.bitwise_and(iq, 1)
    nxt = 1 - cur
    head = pl.program_id(0) * pl.num_programs(1) + pl.program_id(1)

    def first_block(i):
        thr = thr_ref[head * nq + i]
        count = jnp.int32(0)
        for j in range(nkb):
            skip = jnp.logical_and(j < i, thr < ck_ref[head * nkb + j])
            count = count + jnp.where(skip, 1, 0)
        return count

    j0 = first_block(iq)

    ckeys = tk // FOX_CHUNKS
    chunk_rows = [slice(c * ckeys, (c + 1) * ckeys) for c in range(FOX_CHUNKS)]

    def causal(c, s):
        key = lax.broadcasted_iota(jnp.int32, (ckeys, tq), 0) + c * ckeys
        qry = lax.broadcasted_iota(jnp.int32, (ckeys, tq), 1)
        return jnp.where(key <= qry, s, NEG_INF)

    def diagonal_max(s_ref):
        return functools.reduce(jnp.maximum, [jnp.max(causal(c, s_ref[rows, :]), axis=0, keepdims=True)
                                              for c, rows in enumerate(chunk_rows)])

    def step(jn, q_n, sn_ref, mxn_ref, jc=None, sc_ref=None, mx_c=None, diagonal=False):
        if jc is not None:
            m_old = m_ref[...]
            m_new = jnp.maximum(m_old, mx_c)
        mx_parts, pv = [], None
        for c, rows in enumerate(chunk_rows):
            if jc is not None:
                s_c = sc_ref[rows, :]
                p = jnp.exp2((causal(c, s_c) if diagonal else s_c) - m_new).astype(BF16)
            s = _dot(k_ref[0, jn, rows, :], q_n)
            sn_ref[rows, :] = s
            mx_parts.append(jnp.max(s, axis=0, keepdims=True))
            if jc is not None:
                off = (c * ckeys) % tv
                part = _dot(v_ref[0, jc * (tk // tv) + (c * ckeys) // tv, :, off:off + ckeys], p)
                pv = part if pv is None else pv + part
        mxn_ref[...] = functools.reduce(jnp.maximum, mx_parts)
        if jc is not None:
            acc_ref[...] = jnp.exp2(m_old - m_new) * acc_ref[...] + pv
            m_ref[...] = m_new

    def diagonal_and_handover(s_ref):
        step(first_block(jnp.minimum(iq + 1, nq - 1)), contraction_rows(qnext_ref[0]), sf_ref.at[nxt], mxf_ref.at[nxt],
             iq, s_ref, diagonal_max(s_ref), diagonal=True)

    def exact_max_path():
        @pl.when(iq == 0)
        def _():
            step(0, qT, sf_ref.at[cur], mxf_ref.at[cur])

        @pl.when(j0 == iq)
        def _():
            diagonal_and_handover(sf_ref.at[cur])

        @pl.when(j0 < iq)
        def _():
            step(j0 + 1, qT, s1_ref, mx1_ref, j0, sf_ref.at[cur], mxf_ref[cur])
            rest = iq - j0 - 1

            def pair(t, carry):
                j = j0 + 2 * t + 1
                step(j + 1, qT, s0_ref, mx0_ref, j, s1_ref, mx1_ref[...])
                step(j + 2, qT, s1_ref, mx1_ref, j + 1, s0_ref, mx0_ref[...])
                return carry

            lax.fori_loop(0, lax.shift_right_logical(rest, 1), pair, 0)
            odd = lax.bitwise_and(rest, 1)

            @pl.when(odd == 0)
            def _():
                diagonal_and_handover(s1_ref)

            @pl.when(odd == 1)
            def _():
                step(iq, qT, s0_ref, mx0_ref, iq - 1, s1_ref, mx1_ref[...])
                diagonal_and_handover(s0_ref)

    def bound_path():
        q_norm = jnp.sqrt(qn2_ref[0, 0]) * NORM_MARGIN
        c_t = crow_ref[0, 0]

        def pv_chunk(j, c, p):
            off = (c * ckeys) % tv
            return _dot(v_ref[0, j * (tk // tv) + (c * ckeys) // tv, :, off:off + ckeys], p)

        def run_blocks(blocks):
            m = m_ref[...]
            stabiliser = []
            for _, bound, _ in blocks:
                m_new = jnp.maximum(m, bound)
                stabiliser.append((m, m_new))
                m = m_new
            m_ref[...] = m
            stream = [(b, c) for b in range(len(blocks)) for c in range(FOX_CHUNKS)]
            weights, pv = {}, [None] * len(blocks)

            def finish(b, c):
                w, q0 = weights.pop((b, c))
                part = pv_chunk(blocks[b][0], c, w)
                if q0:
                    part = jnp.concatenate([jnp.zeros((part.shape[0], q0), F32), part], axis=1)
                pv[b] = part if pv[b] is None else pv[b] + part
                if c == FOX_CHUNKS - 1:
                    m_old, m_new = stabiliser[b]
                    acc_ref[...] = jnp.exp2(m_old - m_new) * acc_ref[...] + pv[b]

            for i, (b, c) in enumerate(stream):
                j, _, diagonal = blocks[b]
                q0 = c * ckeys if diagonal else 0
                s = _dot(k_ref[0, j, chunk_rows[c], :], qT[:, q0:])
                if diagonal:
                    key = lax.broadcasted_iota(jnp.int32, s.shape, 0)
                    qry = lax.broadcasted_iota(jnp.int32, s.shape, 1)
                    s = jnp.where(key <= qry, s, NEG_INF)
                weights[b, c] = jnp.exp2(s - stabiliser[b][1][:, q0:]).astype(BF16), q0
                if i >= PV_LAG:
                    finish(*stream[i - PV_LAG])
            for item in stream[-PV_LAG:]:
                finish(*item)

        def earlier(j):
            return j, q_norm * kn_ref[head * nkb + j] + (c_t - ck_ref[head * nkb + j]), False

        diagonal = (iq, q_norm * kn_ref[head * nkb + iq], True)

        def group(t, carry):
            run_blocks([earlier(j0 + FOX_GROUP * t + b) for b in range(FOX_GROUP)])
            return carry

        n_earlier = iq - j0
        n_groups = lax.shift_right_logical(n_earlier, FOX_GROUP.bit_length() - 1)
        lax.fori_loop(0, n_groups, group, 0)
        left = n_earlier - FOX_GROUP * n_groups

        @pl.when(left >= 2)
        def _():
            run_blocks([earlier(j0 + FOX_GROUP * n_groups), earlier(j0 + FOX_GROUP * n_groups + 1)])

        odd = lax.bitwise_and(left, 1)

        @pl.when(odd == 0)
        def _():
            run_blocks([diagonal])

        @pl.when(odd == 1)
        def _():
            run_blocks([earlier(iq - 1), diagonal])

    use_bound = slow_ref[head] == 0
    pl.when(use_bound)(bound_path)
    pl.when(jnp.logical_not(use_bound))(exact_max_path)

    acc = acc_ref[...]
    o_ref[0] = (acc[:HEAD_DIM] / acc[HEAD_DIM:HEAD_DIM + 1]).astype(o_ref.dtype)


def _fox_skip_tables(kstats, qstats):
    r = FOX_BLOCK // TOKEN_TILE
    B, nt = kstats.shape[:2]
    H = FOX_HEADS
    per_block = lambda a: a.reshape(B, nt // r, r, H)
    ck = per_block(kstats[:, :, 0, :H])[:, :, r - 1]
    cq = per_block(kstats[:, :, 1, :H])[:, :, 0]
    k_norm = jnp.sqrt(jnp.max(kstats[:, :, 2, :H], axis=1))
    q_norm = jnp.sqrt(jnp.max(per_block(qstats[:, :, :H, 0]), axis=2))
    gap = 2.0 * NORM_MARGIN * q_norm * k_norm[:, None, :]
    thr = cq + gap + SKIP_LOG2
    kn = jnp.sqrt(jnp.max(per_block(kstats[:, :, 2, :H]), axis=2)) * NORM_MARGIN
    slow = jnp.logical_not(jnp.max(gap, axis=1) < BOUND_LIMIT).astype(jnp.int32)
    flat = lambda a: jnp.transpose(a, (0, 2, 1)).reshape(-1).astype(F32)
    return flat(thr), flat(ck), flat(kn), slow.reshape(-1)


def _fox(kf, qf, vf, kstats, qstats, q_norm2, c_rows):
    B, H, nv, tv, _ = kf.shape
    S = qf.shape[2]
    tq = tk = FOX_BLOCK
    kf = kf.reshape(B * H, S // tk, tk, HEAD_PAD)
    vf = vf.reshape(B * H, nv, V_ROWS, tv)
    nq = S // tq
    tables = _fox_skip_tables(kstats, qstats)
    per_query = lambda a: a.reshape(B, a.shape[1], 1, S)
    row_spec = pl.BlockSpec((1, 1, 1, tq), lambda b, h, i, *_: (b, h, 0, i))
    grid_spec = pltpu.PrefetchScalarGridSpec(
        num_scalar_prefetch=len(tables),
        grid=(B, H, nq),
        in_specs=[
            pl.BlockSpec((1, Q_ROWS, tq), lambda b, h, i, *_: (b, h, i)),
            pl.BlockSpec((1, Q_ROWS, tq), lambda b, h, i, *_: (b, h, jnp.minimum(i + 1, nq - 1))),
            row_spec,
            row_spec,
            pl.BlockSpec((1, S // tk, tk, HEAD_PAD), lambda b, h, i, *_: (b * H + h, 0, 0, 0)),
            pl.BlockSpec((1, nv, V_ROWS, tv), lambda b, h, i, *_: (b * H + h, 0, 0, 0)),
        ],
        out_specs=pl.BlockSpec((1, HEAD_DIM, tq), lambda b, h, i, *_: (b, h, i)),
        scratch_shapes=[pltpu.VMEM((tk, tq), F32), pltpu.VMEM((tk, tq), F32),
                        pltpu.VMEM((1, tq), F32), pltpu.VMEM((1, tq), F32),
                        pltpu.VMEM((2, tk, tq), F32), pltpu.VMEM((2, 1, tq), F32),
                        pltpu.VMEM((V_ROWS, tq), F32), pltpu.VMEM((1, tq), F32)],
    )
    return pl.pallas_call(
        _fox_kernel,
        grid_spec=grid_spec,
        out_shape=jax.ShapeDtypeStruct((B, H * HEAD_DIM, S), BF16),
        compiler_params=_params(("arbitrary", "arbitrary", "arbitrary")),
        name="fox",
    )(*tables, qf, qf, per_query(q_norm2), per_query(c_rows), kf, vf)


BAND_BLOCKS = LEFT_CHUNKS * CHUNK // LANES


def _bias_block_id(delta):
    if delta < 0 or delta > BAND_BLOCKS:
        return None
    return {0: 0, 1: 1, BAND_BLOCKS: 3}.get(delta, 2)


def _chunk_kernel(q_ref, kp_ref, kc_ref, vp_ref, vc_ref, blk_ref, o_ref):
    iq = pl.program_id(1)
    tq = q_ref.shape[2]
    nb = tq // LANES
    before_start = jnp.where(iq > 0, 0.0, NEG_INF).astype(F32)
    zero_blk = jnp.zeros((LANES, LANES), BF16)
    row = lax.broadcasted_iota(jnp.int32, (2 * HEAD_DIM, tq), 0)

    def scores(h):
        pair, odd = divmod(h, 2)
        lanes = slice(pair * 2 * HEAD_DIM, (pair + 1) * 2 * HEAD_DIM)
        q_pair = q_ref[0, lanes, :]
        qT = jnp.where((row >= HEAD_DIM) == bool(odd), q_pair, jnp.zeros_like(q_pair))
        return _dot(kp_ref[0, 0, :, lanes], qT), _dot(kc_ref[0, 0, :, lanes], qT)

    ahead = [scores(0), scores(1)]
    for h in range(CHUNK_HEADS):
        s_half = ahead.pop(0)
        if h + 2 < CHUNK_HEADS:
            ahead.append(scores(h + 2))
        p_blocks = {}
        for b in range(nb):
            cols = slice(b * LANES, (b + 1) * LANES)
            entries = []
            for half in (0, 1):
                for al in range(nb):
                    bid = _bias_block_id(b - (al - nb * (1 - half)))
                    if bid is None:
                        continue
                    sb = s_half[half][al * LANES:(al + 1) * LANES, cols] + blk_ref[h, bid]
                    if half == 0:
                        sb = sb + before_start
                    entries.append((half, al, sb))
            m = functools.reduce(jnp.maximum, [jnp.max(sb, axis=0, keepdims=True) for _, _, sb in entries])
            for half, al, sb in entries:
                p_blocks[half, al, b] = jnp.exp2(sb - m).astype(BF16)
        p_half = [jnp.concatenate([jnp.concatenate([p_blocks.get((half, al, b), zero_blk) for b in range(nb)],
                                                   axis=1) for al in range(nb)], axis=0) for half in (0, 1)]
        vrows = slice(h * V_ROWS, (h + 1) * V_ROWS)
        acc = _dot(vc_ref[0, 0, vrows, :], p_half[1]) + _dot(vp_ref[0, 0, vrows, :], p_half[0])
        o_ref[0, h * HEAD_DIM:(h + 1) * HEAD_DIM, :] = (
            acc[:HEAD_DIM] / acc[HEAD_DIM:HEAD_DIM + 1]).astype(o_ref.dtype)


def _chunk(kc, qc, vc, blocks):
    B, nk, tk, hc = kc.shape
    S = qc.shape[2]
    hv = vc.shape[2]
    tq = tk
    assert tq == LEFT_CHUNKS * CHUNK and 2 * CHUNK == LANES
    prev = lambda i: jnp.maximum(i - 1, 0)
    return pl.pallas_call(
        _chunk_kernel,
        grid=(B, S // tq),
        in_specs=[
            pl.BlockSpec((1, hc, tq), lambda b, i: (b, 0, i)),
            pl.BlockSpec((1, 1, tk, hc), lambda b, i: (b, prev(i), 0, 0)),
            pl.BlockSpec((1, 1, tk, hc), lambda b, i: (b, i, 0, 0)),
            pl.BlockSpec((1, 1, hv, tk), lambda b, i: (b, prev(i), 0, 0)),
            pl.BlockSpec((1, 1, hv, tk), lambda b, i: (b, i, 0, 0)),
            _const_spec(blocks.shape),
        ],
        out_specs=pl.BlockSpec((1, hc, tq), lambda b, i: (b, 0, i)),
        out_shape=jax.ShapeDtypeStruct((B, hc, S), BF16),
        compiler_params=_params(("arbitrary", "arbitrary")),
        name="chunk",
    )(qc, kc, kc, vc, vc, blocks)


def _chunk_bias_blocks(rel_bias):
    heads = rel_bias.shape[0]
    n = LANES
    span = 3 * n - 1
    dist = np.arange(span) - (n - 1)
    by_dist = rel_bias.astype(F32)[:, np.clip(dist, -(CHUNK - 1), REL_CLIP) + (CHUNK - 1)] * LOG2E
    padded = jnp.pad(by_dist, ((0, 0), (0, 1)))
    skew = jnp.broadcast_to(padded[:, None, :], (heads, n, span + 1)).reshape(heads, n * (span + 1))
    near = skew[:, :n * span].reshape(heads, n, span)[:, :, n - 1:]
    key_chunk = (np.arange(n) // CHUNK)[:, None]
    qry_chunk = (np.arange(n) // CHUNK)[None, :]
    clipped = jnp.broadcast_to(by_dist[:, -1][:, None, None], (heads, n, n))
    blocks = [
        jnp.where((qry_chunk >= key_chunk)[None], near[:, :, :n], NEG_INF),
        near[:, :, n:],
        clipped,
        jnp.where((2 * BAND_BLOCKS + qry_chunk - key_chunk <= LEFT_CHUNKS)[None], clipped, NEG_INF),
    ]
    return jnp.stack(blocks, axis=1)


def _memkv_kernel(mem_ref, g_ref, w_ref, kv_ref):
    m = _rms_rows(mem_ref[0], g_ref[...]).astype(BF16)
    kv_ref[0] = _dot(m, w_ref[...]).astype(BF16)


def _memkv(mem, g, w):
    B, N, D = mem.shape
    return pl.pallas_call(
        _memkv_kernel,
        grid=(B,),
        in_specs=[pl.BlockSpec((1, N, D), lambda b: (b, 0, 0)), _const_spec(g.shape), _const_spec(w.shape)],
        out_specs=pl.BlockSpec((1, N, 2 * D), lambda b: (b, 0, 0)),
        out_shape=jax.ShapeDtypeStruct((B, N, 2 * D), BF16),
        compiler_params=_params(("arbitrary",)),
        name="memkv",
    )(mem, g, w)


def _mix_mem_kernel(x_ref, of_ref, oc_ref, kv_ref, gfo_ref, gco_ref, wout_ref, gmp_ref,
                    gmq_ref, wq_ref, wo_ref, gmo_ref, y_ref):
    tm, d = x_ref.shape[1:]
    dh = d // MEM_HEADS
    kv = kv_ref[0]
    groups = [slice(r * tm // MIX_GROUPS, (r + 1) * tm // MIX_GROUPS) for r in range(MIX_GROUPS)]

    def mixer_input(rows):
        of = jnp.transpose(of_ref[0, :, rows].astype(F32))
        oc = jnp.transpose(oc_ref[0, :, rows].astype(F32))
        return jnp.concatenate([_rms_rows(of, gfo_ref[...]), _rms_rows(oc, gco_ref[...])], axis=-1).astype(BF16)

    def cross_attention(q):
        outs = []
        for hh in range(MEM_HEADS):
            sl = slice(hh * dh, (hh + 1) * dh)
            s = _dot_nt(q[:, sl], kv[:, sl])
            p = jnp.exp2(s - jnp.max(s, axis=-1, keepdims=True))
            l = jnp.sum(p, axis=-1, keepdims=True)
            outs.append(_dot(p.astype(BF16), kv[:, d + hh * dh:d + (hh + 1) * dh]) / l)
        return jnp.concatenate(outs, axis=-1).astype(BF16)

    cat = [mixer_input(rows) for rows in groups]
    x1 = [x_ref[0, rows, :] + _rms_rows(_dot(c, wout_ref[...]), gmp_ref[...]) for rows, c in zip(groups, cat)]
    q = [_dot(_rms_rows(v, gmq_ref[...]).astype(BF16), wq_ref[...]).astype(BF16) for v in x1]
    ca = [cross_attention(v) for v in q]
    for rows, v, c in zip(groups, x1, ca):
        y_ref[0, rows, :] = v + _rms_rows(_dot(c, wo_ref[...]), gmo_ref[...])


def _mix_mem(x, of, oc, kv, wp):
    B, S, D = x.shape
    tm = MIX_GROUPS * TOKEN_TILE
    half = of.shape[1]
    weights = [wp["g_fox_out"], wp["g_chunk_out"], wp["w_out"], wp["g_mix_post"],
               wp["g_mem_pre"], wp["w_mem_q"], wp["w_mem_o"], wp["g_mem_post"]]
    return pl.pallas_call(
        _mix_mem_kernel,
        grid=(B, S // tm),
        in_specs=[
            pl.BlockSpec((1, tm, D), lambda b, i: (b, i, 0)),
            pl.BlockSpec((1, half, tm), lambda b, i: (b, 0, i)),
            pl.BlockSpec((1, half, tm), lambda b, i: (b, 0, i)),
            pl.BlockSpec((1,) + kv.shape[1:], lambda b, i: (b, 0, 0)),
        ] + [_const_spec(w.shape) for w in weights],
        out_specs=pl.BlockSpec((1, tm, D), lambda b, i: (b, i, 0)),
        out_shape=jax.ShapeDtypeStruct((B, S, D), F32),
        compiler_params=_params(("arbitrary", "arbitrary")),
        name="mix_mem",
    )(x, of, oc, kv, *weights)


FFN_CHUNK = 1024


def _ffn_kernel(x_ref, gpre_ref, wg_ref, wu_ref, wd_ref, gpost_ref, y_ref, act_ref):
    x = x_ref[0]
    h = _rms_rows(x, gpre_ref[...]).astype(BF16)
    dff = wg_ref.shape[1]
    for c0 in range(0, dff, FFN_CHUNK):
        c1 = min(c0 + FFN_CHUNK, dff)
        g = _dot(h, wg_ref[:, c0:c1])
        u = _dot(h, wu_ref[:, c0:c1])
        act_ref[:, c0:c1] = (jax.nn.silu(g) * u).astype(BF16)
    y_ref[0] = x + _rms_rows(_dot(act_ref[...], wd_ref[...]), gpost_ref[...])


def _ffn(x, wp):
    B, S, D = x.shape
    tm = TOKEN_TILE
    weights = [wp["g_ffn_pre"], wp["w_gate"], wp["w_up"], wp["w_down"], wp["g_ffn_post"]]
    dff = wp["w_gate"].shape[1]
    return pl.pallas_call(
        _ffn_kernel,
        grid=(B, S // tm),
        in_specs=[pl.BlockSpec((1, tm, D), lambda b, i: (b, i, 0))] + [_const_spec(w.shape) for w in weights],
        out_specs=pl.BlockSpec((1, tm, D), lambda b, i: (b, i, 0)),
        out_shape=jax.ShapeDtypeStruct((B, S, D), F32),
        scratch_shapes=[pltpu.VMEM((tm, dff), BF16)],
        compiler_params=_params(("arbitrary", "arbitrary")),
        name="ffn",
    )(x, *weights)


def _pad_heads_cols(w, heads, width):
    d = w.shape[0]
    w = w.reshape(d, heads, HEAD_DIM)
    w = jnp.pad(w, ((0, 0), (0, 0), (0, width - HEAD_DIM)))
    return w.reshape(d, heads * width)


def _placement(entries, rows, cols):
    m = np.zeros((rows, cols), np.float32)
    for r, c, v in entries:
        m[r, c] = v
    return jnp.asarray(m, BF16)


def _prep_layer(l, g_mix_pre, w_in, b_fgate, g_fox_out, g_chunk_out, w_out, g_mix_post, g_mem_pre, g_mem_kv,
                w_mem_q, w_mem_kv, w_mem_o, g_mem_post, g_ffn_pre, w_gate_up, w_down, g_ffn_post):
    d = w_in.shape[1]
    fw = FOX_HEADS * HEAD_DIM
    cw = CHUNK_HEADS * HEAD_DIM
    w = w_in[l]
    o = 0
    wq_f, wk_f, wv_f = w[:, o:o + fw], w[:, o + fw:o + 2 * fw], w[:, o + 2 * fw:o + 3 * fw]
    o += 3 * fw
    w_f = w[:, o:o + FOX_HEADS]
    o += FOX_HEADS
    wq_c, wk_c, wv_c = w[:, o:o + cw], w[:, o + cw:o + 2 * cw], w[:, o + 2 * cw:o + 3 * cw]
    scale = HEAD_DIM ** -0.5 * LOG2E

    row = lambda v: v.astype(F32)[None, :]
    wf3 = jnp.pad(jnp.tile(w_f, (1, N_SPLIT)), ((0, 0), (0, LANES - N_SPLIT * FOX_HEADS)))
    bf3 = jnp.pad(jnp.tile(b_fgate[l].astype(F32), N_SPLIT), (0, LANES - N_SPLIT * FOX_HEADS))[None, :]

    ek, eq = [], []
    for h in range(FOX_HEADS):
        for i in range(N_SPLIT):
            gate_lane = (h // 2) * HEAD_PAD + (HEAD_DIM if h % 2 == 0 else 0)
            ek.append((ONES_LANE, gate_lane + i, 1.0))
            ek.append((i * FOX_HEADS + h, gate_lane + N_SPLIT + i, -1.0))
            eq.append((h * Q_ROWS + HEAD_DIM + i, i * FOX_HEADS + h, 1.0))
            eq.append((h * Q_ROWS + HEAD_DIM + N_SPLIT + i, ONES_LANE, 1.0))
    gk = [(h * HEAD_DIM + d, h, 1.0) for h in range(FOX_HEADS) for d in range(HEAD_DIM)]
    gq = [(h, h * Q_ROWS + d, 1.0) for h in range(FOX_HEADS) for d in range(HEAD_DIM)]
    ones_rows = np.zeros((FOX_HEADS * V_ROWS, LANES), np.float32)
    ones_rows[np.arange(FOX_HEADS) * V_ROWS + HEAD_DIM] = 1.0

    dff = w_gate_up.shape[2] // 2
    mem_scale = (w_mem_q.shape[2] // MEM_HEADS) ** -0.5 * LOG2E
    return dict(
        g_mix_pre=row(g_mix_pre[l]),
        wf=wf3.astype(BF16), bf=bf3,
        wkf=wk_f.astype(BF16),
        wqf=_pad_heads_cols(wq_f * scale, FOX_HEADS, Q_ROWS).T.astype(BF16),
        wvf=_pad_heads_cols(wv_f, FOX_HEADS, V_ROWS).T.astype(BF16),
        wkc=wk_c.astype(BF16),
        wqc=(wq_c * scale).T.astype(BF16),
        wvc=_pad_heads_cols(wv_c, CHUNK_HEADS, V_ROWS).T.astype(BF16),
        ek=_placement(ek, LANES, FOX_HEADS * HEAD_DIM),
        eq=_placement(eq, FOX_HEADS * Q_ROWS, LANES),
        ones_rows=jnp.asarray(ones_rows),
        gk=_placement(gk, FOX_HEADS * HEAD_DIM, LANES),
        gq=_placement(gq, 2 * FOX_HEADS, FOX_HEADS * Q_ROWS),
        g_fox_out=row(g_fox_out[l]), g_chunk_out=row(g_chunk_out[l]),
        w_out=w_out[l].astype(BF16), g_mix_post=row(g_mix_post[l]),
        g_mem_pre=row(g_mem_pre[l]), g_mem_kv=row(g_mem_kv[l]),
        w_mem_q=(w_mem_q[l] * mem_scale).astype(BF16), w_mem_kv=w_mem_kv[l].astype(BF16),
        w_mem_o=w_mem_o[l].astype(BF16), g_mem_post=row(g_mem_post[l]),
        g_ffn_pre=row(g_ffn_pre[l]),
        w_gate=w_gate_up[l][:, :dff].astype(BF16), w_up=w_gate_up[l][:, dff:].astype(BF16),
        w_down=w_down[l].astype(BF16), g_ffn_post=row(g_ffn_post[l]),
    )


def kernel(x, mem, g_mix_pre, w_in, b_fgate, rel_bias, g_fox_out, g_chunk_out, w_out, g_mix_post, g_mem_pre,
           g_mem_kv, w_mem_q, w_mem_kv, w_mem_o, g_mem_post, g_ffn_pre, w_gate_up, w_down, g_ffn_post):
    depth = w_in.shape[0]
    for l in range(depth):
        wp = _prep_layer(l, g_mix_pre, w_in, b_fgate, g_fox_out, g_chunk_out, w_out, g_mix_post, g_mem_pre,
                         g_mem_kv, w_mem_q, w_mem_kv, w_mem_o, g_mem_post, g_ffn_pre, w_gate_up, w_down,
                         g_ffn_post)
        kf, qf, vf, kc, qc, vc, kstats, qstats, q_norm2, c_rows = _inproj(x, wp["g_mix_pre"], wp)
        o_f = _fox(kf, qf, vf, kstats, qstats, q_norm2, c_rows)
        o_c = _chunk(kc, qc, vc, _chunk_bias_blocks(rel_bias[l]))
        kv = _memkv(mem, wp["g_mem_kv"], wp["w_mem_kv"])
        x = _mix_mem(x, o_f, o_c, kv, wp)
        x = _ffn(x, wp)
    return x
```

```python
import functools

import jax
import jax.numpy as jnp
import numpy as np
from jax import lax
from jax.experimental import pallas as pl
from jax.experimental.pallas import tpu as pltpu

F32 = jnp.float32
BF16 = jnp.bfloat16

HEAD_DIM = 64
FOX_HEADS = 8
CHUNK_HEADS = 8
CHUNK = 64
LEFT_CHUNKS = 8
REL_CLIP = 128
MEM_HEADS = 4
EPS = 1e-6
NEG_INF = -1e30

LANES = 128
HEAD_PAD = 128
V_ROWS = 80
Q_ROWS = 80
ONES_LANE = LANES - 1
N_SPLIT = 3
TOKEN_TILE = 512
FOX_BLOCK = 1024
FOX_CHUNKS = 4
FOX_GROUP = 4
PV_LAG = 2
MIX_GROUPS = 2
LOG2E = 1.4426950408889634
SKIP_LOG2 = 150.0
NORM_MARGIN = 1.02
BOUND_LIMIT = 100.0
VMEM_LIMIT = 56 * 1024 * 1024


def _params(semantics):
    return pltpu.CompilerParams(dimension_semantics=semantics, vmem_limit_bytes=VMEM_LIMIT)


def _const_spec(shape):
    nd = len(shape)
    return pl.BlockSpec(shape, lambda *_: (0,) * nd, pipeline_mode=pl.Buffered(1))


def _rms_rows(x, g):
    ms = jnp.mean(x * x, axis=-1, keepdims=True)
    return x * lax.rsqrt(ms + EPS) * g


def _split3_lanes(v):
    lane = lax.broadcasted_iota(jnp.int32, v.shape, 1)
    t1 = v.astype(BF16)
    r1 = v - t1.astype(F32)
    t2 = r1.astype(BF16)
    r2 = r1 - t2.astype(F32)
    t3 = r2.astype(BF16)
    return jnp.where(lane < FOX_HEADS, t1, jnp.where(lane < 2 * FOX_HEADS, t2, t3))


def _dot(a, b):
    return jnp.dot(a, b, preferred_element_type=F32)


def _dot_nt(a, b):
    return lax.dot_general(a, b, (((1,), (1,)), ((), ())), preferred_element_type=F32)


def _inproj_kernel(x_ref, g_ref, wf_ref, bf_ref, wkf_ref, wqf_ref, wvf_ref, wkc_ref, wqc_ref, wvc_ref,
                   ek_ref, eq_ref, ones_ref, gk_ref, gq_ref,
                   kf_ref, qf_ref, vf_ref, kc_ref, qc_ref, vc_ref, ks_ref, qs_ref, qn_ref, cr_ref, carry_ref):
    tm = x_ref.shape[1]

    @pl.when(pl.program_id(1) == 0)
    def _():
        carry_ref[...] = jnp.zeros_like(carry_ref)

    h = _rms_rows(x_ref[0], g_ref[...]).astype(BF16)

    z = _dot(h, wf_ref[...]) + bf_ref[...]

    ones_rows = jnp.tile(ones_ref[...], (1, tm // LANES))
    vf = (_dot_nt(wvf_ref[...], h) + ones_rows).astype(BF16)
    for hh in range(FOX_HEADS):
        vf_ref[0, hh, 0] = vf[hh * V_ROWS:(hh + 1) * V_ROWS]
    kc_ref[0, 0] = _dot(h, wkc_ref[...]).astype(BF16)
    qc_ref[0] = _dot_nt(wqc_ref[...], h).astype(BF16)
    vc_ref[0, 0] = (_dot_nt(wvc_ref[...], h) + ones_rows).astype(BF16)
    k_pairs = _dot(h, wkf_ref[...]).astype(BF16).astype(F32)
    q_rows = _dot_nt(wqf_ref[...], h)

    lf = jnp.minimum(z, 0.0) - jnp.log1p(jnp.exp(-jnp.abs(z)))
    row = lax.broadcasted_iota(jnp.int32, (tm, tm), 0)
    col = lax.broadcasted_iota(jnp.int32, (tm, tm), 1)
    tri = (col <= row).astype(BF16)
    l1 = lf.astype(BF16)
    l2 = (lf - l1.astype(F32)).astype(BF16)
    c = _dot(tri, l1) + _dot(tri, l2) + carry_ref[...]
    carry_ref[...] = c[tm - 1:tm, :]

    lane = lax.broadcasted_iota(jnp.int32, c.shape, 1)
    gate = jnp.where(lane == ONES_LANE, jnp.ones_like(c).astype(BF16), _split3_lanes(c * LOG2E))

    k_gate = _dot(gate, ek_ref[...])
    feature_lane = lax.broadcasted_iota(jnp.int32, (tm, HEAD_PAD), 1) < HEAD_DIM
    k_heads = []
    for hh in range(FOX_HEADS):
        group = slice((hh // 2) * HEAD_PAD, (hh // 2 + 1) * HEAD_PAD)
        pair, pair_gate = k_pairs[:, group], k_gate[:, group]
        if hh % 2:
            pair, pair_gate = pltpu.roll(pair, HEAD_DIM, 1), pltpu.roll(pair_gate, HEAD_DIM, 1)
        k_heads.append(jnp.where(feature_lane, pair, pair_gate))
    for hh in range(FOX_HEADS):
        kf_ref[0, hh, 0] = k_heads[hh].astype(BF16)
    qf = (q_rows + _dot_nt(eq_ref[...], gate)).astype(BF16)
    qf_ref[0] = qf

    qf32 = qf.astype(F32)
    k_norm2 = jnp.max(_dot((k_pairs * k_pairs).astype(BF16), gk_ref[...]), axis=0, keepdims=True)
    q_norm2_rows = _dot(gq_ref[...], (qf32 * qf32).astype(BF16))
    q_norm2 = jnp.max(q_norm2_rows, axis=1, keepdims=True)
    c2 = c * LOG2E
    srow = lax.broadcasted_iota(jnp.int32, ks_ref.shape[2:], 0)
    ks_ref[0, 0] = jnp.where(srow == 0, c2[tm - 1:tm, :], jnp.where(srow == 1, c2[0:1, :], k_norm2))
    qs_ref[0, 0] = jnp.broadcast_to(q_norm2, qs_ref.shape[2:])
    qn_ref[0] = q_norm2_rows
    cr_ref[0] = jnp.transpose(c2)[:qn_ref.shape[1], :]


def _inproj(x, g, wp):
    B, S, D = x.shape
    tm = TOKEN_TILE
    nk = S // tm
    hc = CHUNK_HEADS * HEAD_DIM
    hv = FOX_HEADS * V_ROWS
    tile = lambda b, i: (b, i, 0)
    out_shapes = (
        jax.ShapeDtypeStruct((B, FOX_HEADS, nk, tm, HEAD_PAD), BF16),
        jax.ShapeDtypeStruct((B, FOX_HEADS * Q_ROWS, S), BF16),
        jax.ShapeDtypeStruct((B, FOX_HEADS, nk, V_ROWS, tm), BF16),
        jax.ShapeDtypeStruct((B, nk, tm, hc), BF16),
        jax.ShapeDtypeStruct((B, hc, S), BF16),
        jax.ShapeDtypeStruct((B, nk, hv, tm), BF16),
        jax.ShapeDtypeStruct((B, nk, 8, LANES), F32),
        jax.ShapeDtypeStruct((B, nk, 2 * FOX_HEADS, LANES), F32),
        jax.ShapeDtypeStruct((B, 2 * FOX_HEADS, S), F32),
        jax.ShapeDtypeStruct((B, 2 * FOX_HEADS, S), F32),
    )
    stat_spec = lambda r: pl.BlockSpec((1, 1, r, LANES), lambda b, i: (b, i, 0, 0))
    k_spec = lambda w: pl.BlockSpec((1, 1, tm, w), lambda b, i: (b, i, 0, 0))
    q_spec = lambda w: pl.BlockSpec((1, w, tm), lambda b, i: (b, 0, i))
    v_spec = pl.BlockSpec((1, 1, hv, tm), lambda b, i: (b, i, 0, 0))
    head_major = lambda r, c: pl.BlockSpec((1, FOX_HEADS, 1, r, c), lambda b, i: (b, 0, i, 0, 0))
    weights = [g, wp["wf"], wp["bf"], wp["wkf"], wp["wqf"], wp["wvf"], wp["wkc"], wp["wqc"], wp["wvc"],
               wp["ek"], wp["eq"], wp["ones_rows"], wp["gk"], wp["gq"]]
    return pl.pallas_call(
        _inproj_kernel,
        grid=(B, nk),
        in_specs=[pl.BlockSpec((1, tm, D), tile)] + [_const_spec(w.shape) for w in weights],
        out_specs=(head_major(tm, HEAD_PAD), q_spec(FOX_HEADS * Q_ROWS), head_major(V_ROWS, tm),
                   k_spec(hc), q_spec(hc), v_spec,
                   stat_spec(8), stat_spec(2 * FOX_HEADS), q_spec(2 * FOX_HEADS), q_spec(2 * FOX_HEADS)),
        out_shape=out_shapes,
        scratch_shapes=[pltpu.VMEM((1, LANES), F32)],
        compiler_params=_params(("arbitrary", "arbitrary")),
        name="inproj",
    )(x, *weights)


def _fox_kernel(thr_ref, ck_ref, kn_ref, slow_ref, q_ref, qnext_ref, qn2_ref, crow_ref, k_ref, v_ref, o_ref,
                s0_ref, s1_ref, mx0_ref, mx1_ref, sf_ref, mxf_ref, acc_ref, m_ref):
    iq = pl.program_id(2)
    nq = pl.num_programs(2)
    tq = q_ref.shape[2]
    nkb, tk = k_ref.shape[1:3]
    tv = v_ref.shape[3]
    def contraction_rows(q):
        return jnp.concatenate([q, jnp.zeros((k_ref.shape[3] - q.shape[0], tq), q.dtype)], axis=0)

    qT = contraction_rows(q_ref[0])
    m_ref[...] = jnp.full(m_ref.shape, NEG_INF, F32)
    acc_ref[...] = jnp.zeros_like(acc_ref)
    cur = lax.bitwise_and(iq, 1)
    nxt = 1 - cur
    head = pl.program_id(0) * pl.num_programs(1) + pl.program_id(1)

    def first_block(i):
        thr = thr_ref[head * nq + i]
        count = jnp.int32(0)
        for j in range(nkb):
            skip = jnp.logical_and(j < i, thr < ck_ref[head * nkb + j])
            count = count + jnp.where(skip, 1, 0)
        return count

    j0 = first_block(iq)

    ckeys = tk // FOX_CHUNKS
    chunk_rows = [slice(c * ckeys, (c + 1) * ckeys) for c in range(FOX_CHUNKS)]

    def causal(c, s):
        key = lax.broadcasted_iota(jnp.int32, (ckeys, tq), 0) + c * ckeys
        qry = lax.broadcasted_iota(jnp.int32, (ckeys, tq), 1)
        return jnp.where(key <= qry, s, NEG_INF)

    def diagonal_max(s_ref):
        return functools.reduce(jnp.maximum, [jnp.max(causal(c, s_ref[rows, :]), axis=0, keepdims=True)
                                              for c, rows in enumerate(chunk_rows)])

    def step(jn, q_n, sn_ref, mxn_ref, jc=None, sc_ref=None, mx_c=None, diagonal=False):
        if jc is not None:
            m_old = m_ref[...]
            m_new = jnp.maximum(m_old, mx_c)
        mx_parts, pv = [], None
        for c, rows in enumerate(chunk_rows):
            if jc is not None:
                s_c = sc_ref[rows, :]
                p = jnp.exp2((causal(c, s_c) if diagonal else s_c) - m_new).astype(BF16)
            s = _dot(k_ref[0, jn, rows, :], q_n)
            sn_ref[rows, :] = s
            mx_parts.append(jnp.max(s, axis=0, keepdims=True))
            if jc is not None:
                off = (c * ckeys) % tv
                part = _dot(v_ref[0, jc * (tk // tv) + (c * ckeys) // tv, :, off:off + ckeys], p)
                pv = part if pv is None else pv + part
        mxn_ref[...] = functools.reduce(jnp.maximum, mx_parts)
        if jc is not None:
            acc_ref[...] = jnp.exp2(m_old - m_new) * acc_ref[...] + pv
            m_ref[...] = m_new

    def diagonal_and_handover(s_ref):
        step(first_block(jnp.minimum(iq + 1, nq - 1)), contraction_rows(qnext_ref[0]), sf_ref.at[nxt], mxf_ref.at[nxt],
             iq, s_ref, diagonal_max(s_ref), diagonal=True)

    def exact_max_path():
        @pl.when(iq == 0)
        def _():
            step(0, qT, sf_ref.at[cur], mxf_ref.at[cur])

        @pl.when(j0 == iq)
        def _():
            diagonal_and_handover(sf_ref.at[cur])

        @pl.when(j0 < iq)
        def _():
            step(j0 + 1, qT, s1_ref, mx1_ref, j0, sf_ref.at[cur], mxf_ref[cur])
            rest = iq - j0 - 1

            def pair(t, carry):
                j = j0 + 2 * t + 1
                step(j + 1, qT, s0_ref, mx0_ref, j, s1_ref, mx1_ref[...])
                step(j + 2, qT, s1_ref, mx1_ref, j + 1, s0_ref, mx0_ref[...])
                return carry

            lax.fori_loop(0, lax.shift_right_logical(rest, 1), pair, 0)
            odd = lax.bitwise_and(rest, 1)

            @pl.when(odd == 0)
            def _():
                diagonal_and_handover(s1_ref)

            @pl.when(odd == 1)
            def _():
                step(iq, qT, s0_ref, mx0_ref, iq - 1, s1_ref, mx1_ref[...])
                diagonal_and_handover(s0_ref)

    def bound_path():
        q_norm = jnp.sqrt(qn2_ref[0, 0]) * NORM_MARGIN
        c_t = crow_ref[0, 0]

        def pv_chunk(j, c, p):
            off = (c * ckeys) % tv
            return _dot(v_ref[0, j * (tk // tv) + (c * ckeys) // tv, :, off:off + ckeys], p)

        def run_blocks(blocks):
            m = m_ref[...]
            stabiliser = []
            for _, bound, _ in blocks:
                m_new = jnp.maximum(m, bound)
                stabiliser.append((m, m_new))
                m = m_new
            m_ref[...] = m
            stream = [(b, c) for b in range(len(blocks)) for c in range(FOX_CHUNKS)]
            weights, pv = {}, [None] * len(blocks)

            def finish(b, c):
                w, q0 = weights.pop((b, c))
                part = pv_chunk(blocks[b][0], c, w)
                if q0:
                    part = jnp.concatenate([jnp.zeros((part.shape[0], q0), F32), part], axis=1)
                pv[b] = part if pv[b] is None else pv[b] + part
                if c == FOX_CHUNKS - 1:
                    m_old, m_new = stabiliser[b]
                    acc_ref[...] = jnp.exp2(m_old - m_new) * acc_ref[...] + pv[b]

            for i, (b, c) in enumerate(stream):
                j, _, diagonal = blocks[b]
                q0 = c * ckeys if diagonal else 0
                s = _dot(k_ref[0, j, chunk_rows[c], :], qT[:, q0:])
                if diagonal:
                    key = lax.broadcasted_iota(jnp.int32, s.shape, 0)
                    qry = lax.broadcasted_iota(jnp.int32, s.shape, 1)
                    s = jnp.where(key <= qry, s, NEG_INF)
                weights[b, c] = jnp.exp2(s - stabiliser[b][1][:, q0:]).astype(BF16), q0
                if i >= PV_LAG:
                    finish(*stream[i - PV_LAG])
            for item in stream[-PV_LAG:]:
                finish(*item)

        def earlier(j):
            return j, q_norm * kn_ref[head * nkb + j] + (c_t - ck_ref[head * nkb + j]), False

        diagonal = (iq, q_norm * kn_ref[head * nkb + iq], True)

        def group(t, carry):
            run_blocks([earlier(j0 + FOX_GROUP * t + b) for b in range(FOX_GROUP)])
            return carry

        n_earlier = iq - j0
        n_groups = lax.shift_right_logical(n_earlier, FOX_GROUP.bit_length() - 1)
        lax.fori_loop(0, n_groups, group, 0)
        left = n_earlier - FOX_GROUP * n_groups

        for r in range(FOX_GROUP):
            @pl.when(left == r)
            def _(r=r):
                run_blocks([earlier(iq - r + b) for b in range(r)] + [diagonal])

    use_bound = slow_ref[head] == 0
    pl.when(use_bound)(bound_path)
    pl.when(jnp.logical_not(use_bound))(exact_max_path)

    acc = acc_ref[...]
    o_ref[0] = (acc[:HEAD_DIM] / acc[HEAD_DIM:HEAD_DIM + 1]).astype(o_ref.dtype)


def _fox_skip_tables(kstats, qstats):
    r = FOX_BLOCK // TOKEN_TILE
    B, nt = kstats.shape[:2]
    H = FOX_HEADS
    per_block = lambda a: a.reshape(B, nt // r, r, H)
    ck = per_block(kstats[:, :, 0, :H])[:, :, r - 1]
    cq = per_block(kstats[:, :, 1, :H])[:, :, 0]
    k_norm = jnp.sqrt(jnp.max(kstats[:, :, 2, :H], axis=1))
    q_norm = jnp.sqrt(jnp.max(per_block(qstats[:, :, :H, 0]), axis=2))
    gap = 2.0 * NORM_MARGIN * q_norm * k_norm[:, None, :]
    thr = cq + gap + SKIP_LOG2
    kn = jnp.sqrt(jnp.max(per_block(kstats[:, :, 2, :H]), axis=2)) * NORM_MARGIN
    slow = jnp.logical_not(jnp.max(gap, axis=1) < BOUND_LIMIT).astype(jnp.int32)
    flat = lambda a: jnp.transpose(a, (0, 2, 1)).reshape(-1).astype(F32)
    return flat(thr), flat(ck), flat(kn), slow.reshape(-1)


def _fox(kf, qf, vf, kstats, qstats, q_norm2, c_rows):
    B, H, nv, tv, _ = kf.shape
    S = qf.shape[2]
    tq = tk = FOX_BLOCK
    kf = kf.reshape(B * H, S // tk, tk, HEAD_PAD)
    vf = vf.reshape(B * H, nv, V_ROWS, tv)
    nq = S // tq
    tables = _fox_skip_tables(kstats, qstats)
    per_query = lambda a: a.reshape(B, a.shape[1], 1, S)
    row_spec = pl.BlockSpec((1, 1, 1, tq), lambda b, h, i, *_: (b, h, 0, i))
    grid_spec = pltpu.PrefetchScalarGridSpec(
        num_scalar_prefetch=len(tables),
        grid=(B, H, nq),
        in_specs=[
            pl.BlockSpec((1, Q_ROWS, tq), lambda b, h, i, *_: (b, h, i)),
            pl.BlockSpec((1, Q_ROWS, tq), lambda b, h, i, *_: (b, h, jnp.minimum(i + 1, nq - 1))),
            row_spec,
            row_spec,
            pl.BlockSpec((1, S // tk, tk, HEAD_PAD), lambda b, h, i, *_: (b * H + h, 0, 0, 0)),
            pl.BlockSpec((1, nv, V_ROWS, tv), lambda b, h, i, *_: (b * H + h, 0, 0, 0)),
        ],
        out_specs=pl.BlockSpec((1, HEAD_DIM, tq), lambda b, h, i, *_: (b, h, i)),
        scratch_shapes=[pltpu.VMEM((tk, tq), F32), pltpu.VMEM((tk, tq), F32),
                        pltpu.VMEM((1, tq), F32), pltpu.VMEM((1, tq), F32),
                        pltpu.VMEM((2, tk, tq), F32), pltpu.VMEM((2, 1, tq), F32),
                        pltpu.VMEM((V_ROWS, tq), F32), pltpu.VMEM((1, tq), F32)],
    )
    return pl.pallas_call(
        _fox_kernel,
        grid_spec=grid_spec,
        out_shape=jax.ShapeDtypeStruct((B, H * HEAD_DIM, S), BF16),
        compiler_params=_params(("arbitrary", "arbitrary", "arbitrary")),
        name="fox",
    )(*tables, qf, qf, per_query(q_norm2), per_query(c_rows), kf, vf)


BAND_BLOCKS = LEFT_CHUNKS * CHUNK // LANES


def _bias_block_id(delta):
    if delta < 0 or delta > BAND_BLOCKS:
        return None
    return {0: 0, 1: 1, BAND_BLOCKS: 3}.get(delta, 2)


def _chunk_kernel(q_ref, kp_ref, kc_ref, vp_ref, vc_ref, blk_ref, o_ref):
    iq = pl.program_id(1)
    tq = q_ref.shape[2]
    nb = tq // LANES
    before_start = jnp.where(iq > 0, 0.0, NEG_INF).astype(F32)
    zero_blk = jnp.zeros((LANES, LANES), BF16)
    row = lax.broadcasted_iota(jnp.int32, (2 * HEAD_DIM, tq), 0)

    def scores(h):
        pair, odd = divmod(h, 2)
        lanes = slice(pair * 2 * HEAD_DIM, (pair + 1) * 2 * HEAD_DIM)
        q_pair = q_ref[0, lanes, :]
        qT = jnp.where((row >= HEAD_DIM) == bool(odd), q_pair, jnp.zeros_like(q_pair))
        return _dot(kp_ref[0, 0, :, lanes], qT), _dot(kc_ref[0, 0, :, lanes], qT)

    ahead = [scores(0), scores(1)]
    for h in range(CHUNK_HEADS):
        s_half = ahead.pop(0)
        if h + 2 < CHUNK_HEADS:
            ahead.append(scores(h + 2))
        p_blocks = {}
        for b in range(nb):
            cols = slice(b * LANES, (b + 1) * LANES)
            entries = []
            for half in (0, 1):
                for al in range(nb):
                    bid = _bias_block_id(b - (al - nb * (1 - half)))
                    if bid is None:
                        continue
                    sb = s_half[half][al * LANES:(al + 1) * LANES, cols] + blk_ref[h, bid]
                    if half == 0:
                        sb = sb + before_start
                    entries.append((half, al, sb))
            m = functools.reduce(jnp.maximum, [jnp.max(sb, axis=0, keepdims=True) for _, _, sb in entries])
            for half, al, sb in entries:
                p_blocks[half, al, b] = jnp.exp2(sb - m).astype(BF16)
        p_half = [jnp.concatenate([jnp.concatenate([p_blocks.get((half, al, b), zero_blk) for b in range(nb)],
                                                   axis=1) for al in range(nb)], axis=0) for half in (0, 1)]
        vrows = slice(h * V_ROWS, (h + 1) * V_ROWS)
        acc = _dot(vc_ref[0, 0, vrows, :], p_half[1]) + _dot(vp_ref[0, 0, vrows, :], p_half[0])
        o_ref[0, h * HEAD_DIM:(h + 1) * HEAD_DIM, :] = (
            acc[:HEAD_DIM] / acc[HEAD_DIM:HEAD_DIM + 1]).astype(o_ref.dtype)


def _chunk(kc, qc, vc, blocks):
    B, nk, tk, hc = kc.shape
    S = qc.shape[2]
    hv = vc.shape[2]
    tq = tk
    assert tq == LEFT_CHUNKS * CHUNK and 2 * CHUNK == LANES
    prev = lambda i: jnp.maximum(i - 1, 0)
    return pl.pallas_call(
        _chunk_kernel,
        grid=(B, S // tq),
        in_specs=[
            pl.BlockSpec((1, hc, tq), lambda b, i: (b, 0, i)),
            pl.BlockSpec((1, 1, tk, hc), lambda b, i: (b, prev(i), 0, 0)),
            pl.BlockSpec((1, 1, tk, hc), lambda b, i: (b, i, 0, 0)),
            pl.BlockSpec((1, 1, hv, tk), lambda b, i: (b, prev(i), 0, 0)),
            pl.BlockSpec((1, 1, hv, tk), lambda b, i: (b, i, 0, 0)),
            _const_spec(blocks.shape),
        ],
        out_specs=pl.BlockSpec((1, hc, tq), lambda b, i: (b, 0, i)),
        out_shape=jax.ShapeDtypeStruct((B, hc, S), BF16),
        compiler_params=_params(("arbitrary", "arbitrary")),
        name="chunk",
    )(qc, kc, kc, vc, vc, blocks)


def _chunk_bias_blocks(rel_bias):
    heads = rel_bias.shape[0]
    n = LANES
    span = 3 * n - 1
    dist = np.arange(span) - (n - 1)
    by_dist = rel_bias.astype(F32)[:, np.clip(dist, -(CHUNK - 1), REL_CLIP) + (CHUNK - 1)] * LOG2E
    padded = jnp.pad(by_dist, ((0, 0), (0, 1)))
    skew = jnp.broadcast_to(padded[:, None, :], (heads, n, span + 1)).reshape(heads, n * (span + 1))
    near = skew[:, :n * span].reshape(heads, n, span)[:, :, n - 1:]
    key_chunk = (np.arange(n) // CHUNK)[:, None]
    qry_chunk = (np.arange(n) // CHUNK)[None, :]
    clipped = jnp.broadcast_to(by_dist[:, -1][:, None, None], (heads, n, n))
    blocks = [
        jnp.where((qry_chunk >= key_chunk)[None], near[:, :, :n], NEG_INF),
        near[:, :, n:],
        clipped,
        jnp.where((2 * BAND_BLOCKS + qry_chunk - key_chunk <= LEFT_CHUNKS)[None], clipped, NEG_INF),
    ]
    return jnp.stack(blocks, axis=1)


def _memkv_kernel(mem_ref, g_ref, w_ref, kv_ref):
    m = _rms_rows(mem_ref[0], g_ref[...]).astype(BF16)
    kv_ref[0] = _dot(m, w_ref[...]).astype(BF16)


def _memkv(mem, g, w):
    B, N, D = mem.shape
    return pl.pallas_call(
        _memkv_kernel,
        grid=(B,),
        in_specs=[pl.BlockSpec((1, N, D), lambda b: (b, 0, 0)), _const_spec(g.shape), _const_spec(w.shape)],
        out_specs=pl.BlockSpec((1, N, 2 * D), lambda b: (b, 0, 0)),
        out_shape=jax.ShapeDtypeStruct((B, N, 2 * D), BF16),
        compiler_params=_params(("arbitrary",)),
        name="memkv",
    )(mem, g, w)


def _mix_mem_kernel(x_ref, of_ref, oc_ref, kv_ref, gfo_ref, gco_ref, wout_ref, gmp_ref,
                    gmq_ref, wq_ref, wo_ref, gmo_ref, y_ref):
    tm, d = x_ref.shape[1:]
    dh = d // MEM_HEADS
    kv = kv_ref[0]
    groups = [slice(r * tm // MIX_GROUPS, (r + 1) * tm // MIX_GROUPS) for r in range(MIX_GROUPS)]

    def mixer_input(rows):
        of = jnp.transpose(of_ref[0, :, rows].astype(F32))
        oc = jnp.transpose(oc_ref[0, :, rows].astype(F32))
        return jnp.concatenate([_rms_rows(of, gfo_ref[...]), _rms_rows(oc, gco_ref[...])], axis=-1).astype(BF16)

    def cross_attention(q):
        outs = []
        for hh in range(MEM_HEADS):
            sl = slice(hh * dh, (hh + 1) * dh)
            s = _dot_nt(q[:, sl], kv[:, sl])
            p = jnp.exp2(s - jnp.max(s, axis=-1, keepdims=True))
            l = jnp.sum(p, axis=-1, keepdims=True)
            outs.append(_dot(p.astype(BF16), kv[:, d + hh * dh:d + (hh + 1) * dh]) / l)
        return jnp.concatenate(outs, axis=-1).astype(BF16)

    cat = [mixer_input(rows) for rows in groups]
    x1 = [x_ref[0, rows, :] + _rms_rows(_dot(c, wout_ref[...]), gmp_ref[...]) for rows, c in zip(groups, cat)]
    q = [_dot(_rms_rows(v, gmq_ref[...]).astype(BF16), wq_ref[...]).astype(BF16) for v in x1]
    ca = [cross_attention(v) for v in q]
    for rows, v, c in zip(groups, x1, ca):
        y_ref[0, rows, :] = v + _rms_rows(_dot(c, wo_ref[...]), gmo_ref[...])


def _mix_mem(x, of, oc, kv, wp):
    B, S, D = x.shape
    tm = MIX_GROUPS * TOKEN_TILE
    half = of.shape[1]
    weights = [wp["g_fox_out"], wp["g_chunk_out"], wp["w_out"], wp["g_mix_post"],
               wp["g_mem_pre"], wp["w_mem_q"], wp["w_mem_o"], wp["g_mem_post"]]
    return pl.pallas_call(
        _mix_mem_kernel,
        grid=(B, S // tm),
        in_specs=[
            pl.BlockSpec((1, tm, D), lambda b, i: (b, i, 0)),
            pl.BlockSpec((1, half, tm), lambda b, i: (b, 0, i)),
            pl.BlockSpec((1, half, tm), lambda b, i: (b, 0, i)),
            pl.BlockSpec((1,) + kv.shape[1:], lambda b, i: (b, 0, 0)),
        ] + [_const_spec(w.shape) for w in weights],
        out_specs=pl.BlockSpec((1, tm, D), lambda b, i: (b, i, 0)),
        out_shape=jax.ShapeDtypeStruct((B, S, D), F32),
        compiler_params=_params(("arbitrary", "arbitrary")),
        name="mix_mem",
    )(x, of, oc, kv, *weights)


FFN_CHUNK = 1024


def _ffn_kernel(x_ref, gpre_ref, wg_ref, wu_ref, wd_ref, gpost_ref, y_ref, act_ref):
    x = x_ref[0]
    h = _rms_rows(x, gpre_ref[...]).astype(BF16)
    dff = wg_ref.shape[1]
    for c0 in range(0, dff, FFN_CHUNK):
        c1 = min(c0 + FFN_CHUNK, dff)
        g = _dot(h, wg_ref[:, c0:c1])
        u = _dot(h, wu_ref[:, c0:c1])
        act_ref[:, c0:c1] = (jax.nn.silu(g) * u).astype(BF16)
    y_ref[0] = x + _rms_rows(_dot(act_ref[...], wd_ref[...]), gpost_ref[...])


def _ffn(x, wp):
    B, S, D = x.shape
    tm = TOKEN_TILE
    weights = [wp["g_ffn_pre"], wp["w_gate"], wp["w_up"], wp["w_down"], wp["g_ffn_post"]]
    dff = wp["w_gate"].shape[1]
    return pl.pallas_call(
        _ffn_kernel,
        grid=(B, S // tm),
        in_specs=[pl.BlockSpec((1, tm, D), lambda b, i: (b, i, 0))] + [_const_spec(w.shape) for w in weights],
        out_specs=pl.BlockSpec((1, tm, D), lambda b, i: (b, i, 0)),
        out_shape=jax.ShapeDtypeStruct((B, S, D), F32),
        scratch_shapes=[pltpu.VMEM((tm, dff), BF16)],
        compiler_params=_params(("arbitrary", "arbitrary")),
        name="ffn",
    )(x, *weights)


def _pad_heads_cols(w, heads, width):
    d = w.shape[0]
    w = w.reshape(d, heads, HEAD_DIM)
    w = jnp.pad(w, ((0, 0), (0, 0), (0, width - HEAD_DIM)))
    return w.reshape(d, heads * width)


def _placement(entries, rows, cols):
    m = np.zeros((rows, cols), np.float32)
    for r, c, v in entries:
        m[r, c] = v
    return jnp.asarray(m, BF16)


def _prep_layer(l, g_mix_pre, w_in, b_fgate, g_fox_out, g_chunk_out, w_out, g_mix_post, g_mem_pre, g_mem_kv,
                w_mem_q, w_mem_kv, w_mem_o, g_mem_post, g_ffn_pre, w_gate_up, w_down, g_ffn_post):
    d = w_in.shape[1]
    fw = FOX_HEADS * HEAD_DIM
    cw = CHUNK_HEADS * HEAD_DIM
    w = w_in[l]
    o = 0
    wq_f, wk_f, wv_f = w[:, o:o + fw], w[:, o + fw:o + 2 * fw], w[:, o + 2 * fw:o + 3 * fw]
    o += 3 * fw
    w_f = w[:, o:o + FOX_HEADS]
    o += FOX_HEADS
    wq_c, wk_c, wv_c = w[:, o:o + cw], w[:, o + cw:o + 2 * cw], w[:, o + 2 * cw:o + 3 * cw]
    scale = HEAD_DIM ** -0.5 * LOG2E

    row = lambda v: v.astype(F32)[None, :]
    wf3 = jnp.pad(jnp.tile(w_f, (1, N_SPLIT)), ((0, 0), (0, LANES - N_SPLIT * FOX_HEADS)))
    bf3 = jnp.pad(jnp.tile(b_fgate[l].astype(F32), N_SPLIT), (0, LANES - N_SPLIT * FOX_HEADS))[None, :]

    ek, eq = [], []
    for h in range(FOX_HEADS):
        for i in range(N_SPLIT):
            gate_lane = (h // 2) * HEAD_PAD + (HEAD_DIM if h % 2 == 0 else 0)
            ek.append((ONES_LANE, gate_lane + i, 1.0))
            ek.append((i * FOX_HEADS + h, gate_lane + N_SPLIT + i, -1.0))
            eq.append((h * Q_ROWS + HEAD_DIM + i, i * FOX_HEADS + h, 1.0))
            eq.append((h * Q_ROWS + HEAD_DIM + N_SPLIT + i, ONES_LANE, 1.0))
    gk = [(h * HEAD_DIM + d, h, 1.0) for h in range(FOX_HEADS) for d in range(HEAD_DIM)]
    gq = [(h, h * Q_ROWS + d, 1.0) for h in range(FOX_HEADS) for d in range(HEAD_DIM)]
    ones_rows = np.zeros((FOX_HEADS * V_ROWS, LANES), np.float32)
    ones_rows[np.arange(FOX_HEADS) * V_ROWS + HEAD_DIM] = 1.0

    dff = w_gate_up.shape[2] // 2
    mem_scale = (w_mem_q.shape[2] // MEM_HEADS) ** -0.5 * LOG2E
    return dict(
        g_mix_pre=row(g_mix_pre[l]),
        wf=wf3.astype(BF16), bf=bf3,
        wkf=wk_f.astype(BF16),
        wqf=_pad_heads_cols(wq_f * scale, FOX_HEADS, Q_ROWS).T.astype(BF16),
        wvf=_pad_heads_cols(wv_f, FOX_HEADS, V_ROWS).T.astype(BF16),
        wkc=wk_c.astype(BF16),
        wqc=(wq_c * scale).T.astype(BF16),
        wvc=_pad_heads_cols(wv_c, CHUNK_HEADS, V_ROWS).T.astype(BF16),
        ek=_placement(ek, LANES, FOX_HEADS * HEAD_DIM),
        eq=_placement(eq, FOX_HEADS * Q_ROWS, LANES),
        ones_rows=jnp.asarray(ones_rows),
        gk=_placement(gk, FOX_HEADS * HEAD_DIM, LANES),
        gq=_placement(gq, 2 * FOX_HEADS, FOX_HEADS * Q_ROWS),
        g_fox_out=row(g_fox_out[l]), g_chunk_out=row(g_chunk_out[l]),
        w_out=w_out[l].astype(BF16), g_mix_post=row(g_mix_post[l]),
        g_mem_pre=row(g_mem_pre[l]), g_mem_kv=row(g_mem_kv[l]),
        w_mem_q=(w_mem_q[l] * mem_scale).astype(BF16), w_mem_kv=w_mem_kv[l].astype(BF16),
        w_mem_o=w_mem_o[l].astype(BF16), g_mem_post=row(g_mem_post[l]),
        g_ffn_pre=row(g_ffn_pre[l]),
        w_gate=w_gate_up[l][:, :dff].astype(BF16), w_up=w_gate_up[l][:, dff:].astype(BF16),
        w_down=w_down[l].astype(BF16), g_ffn_post=row(g_ffn_post[l]),
    )


def kernel(x, mem, g_mix_pre, w_in, b_fgate, rel_bias, g_fox_out, g_chunk_out, w_out, g_mix_post, g_mem_pre,
           g_mem_kv, w_mem_q, w_mem_kv, w_mem_o, g_mem_post, g_ffn_pre, w_gate_up, w_down, g_ffn_post):
    depth = w_in.shape[0]
    for l in range(depth):
        wp = _prep_layer(l, g_mix_pre, w_in, b_fgate, g_fox_out, g_chunk_out, w_out, g_mix_post, g_mem_pre,
                         g_mem_kv, w_mem_q, w_mem_kv, w_mem_o, g_mem_post, g_ffn_pre, w_gate_up, w_down,
                         g_ffn_post)
        kf, qf, vf, kc, qc, vc, kstats, qstats, q_norm2, c_rows = _inproj(x, wp["g_mix_pre"], wp)
        o_f = _fox(kf, qf, vf, kstats, qstats, q_norm2, c_rows)
        o_c = _chunk(kc, qc, vc, _chunk_bias_blocks(rel_bias[l]))
        kv = _memkv(mem, wp["g_mem_kv"], wp["w_mem_kv"])
        x = _mix_mem(x, o_f, o_c, kv, wp)
        x = _ffn(x, wp)
    return x
```

```python
import functools

import jax
import jax.numpy as jnp
import numpy as np
from jax import lax
from jax.experimental import pallas as pl
from jax.experimental.pallas import tpu as pltpu

F32 = jnp.float32
BF16 = jnp.bfloat16

HEAD_DIM = 64
FOX_HEADS = 8
CHUNK_HEADS = 8
CHUNK = 64
LEFT_CHUNKS = 8
REL_CLIP = 128
MEM_HEADS = 4
EPS = 1e-6
NEG_INF = -1e30

LANES = 128
HEAD_PAD = 128
V_ROWS = 80
Q_ROWS = 80
ONES_LANE = LANES - 1
N_SPLIT = 3
TOKEN_TILE = 512
FOX_BLOCK = 1024
FOX_CHUNKS = 4
FOX_GROUP = 4
PV_LAG = 2
MIX_GROUPS = 2
LOG2E = 1.4426950408889634
SKIP_LOG2 = 150.0
NORM_MARGIN = 1.02
BOUND_LIMIT = 100.0
VMEM_LIMIT = 56 * 1024 * 1024


def _params(semantics):
    return pltpu.CompilerParams(dimension_semantics=semantics, vmem_limit_bytes=VMEM_LIMIT)


def _const_spec(shape):
    nd = len(shape)
    return pl.BlockSpec(shape, lambda *_: (0,) * nd, pipeline_mode=pl.Buffered(1))


def _rms_rows(x, g):
    ms = jnp.mean(x * x, axis=-1, keepdims=True)
    return x * lax.rsqrt(ms + EPS) * g


def _split3_lanes(v):
    lane = lax.broadcasted_iota(jnp.int32, v.shape, 1)
    t1 = v.astype(BF16)
    r1 = v - t1.astype(F32)
    t2 = r1.astype(BF16)
    r2 = r1 - t2.astype(F32)
    t3 = r2.astype(BF16)
    return jnp.where(lane < FOX_HEADS, t1, jnp.where(lane < 2 * FOX_HEADS, t2, t3))


def _dot(a, b):
    return jnp.dot(a, b, preferred_element_type=F32)


def _dot_nt(a, b):
    return lax.dot_general(a, b, (((1,), (1,)), ((), ())), preferred_element_type=F32)


def _inproj_kernel(x_ref, g_ref, wf_ref, bf_ref, wkf_ref, wqf_ref, wvf_ref, wkc_ref, wqc_ref, wvc_ref,
                   ek_ref, eq_ref, ones_ref, gk_ref, gq_ref,
                   kf_ref, qf_ref, vf_ref, kc_ref, qc_ref, vc_ref, ks_ref, qs_ref, qn_ref, cr_ref, carry_ref):
    tm = x_ref.shape[1]

    @pl.when(pl.program_id(1) == 0)
    def _():
        carry_ref[...] = jnp.zeros_like(carry_ref)

    h = _rms_rows(x_ref[0], g_ref[...]).astype(BF16)

    z = _dot(h, wf_ref[...]) + bf_ref[...]

    ones_rows = jnp.tile(ones_ref[...], (1, tm // LANES))
    vf = (_dot_nt(wvf_ref[...], h) + ones_rows).astype(BF16)
    for hh in range(FOX_HEADS):
        vf_ref[0, hh, 0] = vf[hh * V_ROWS:(hh + 1) * V_ROWS]
    kc_ref[0, 0] = _dot(h, wkc_ref[...]).astype(BF16)
    qc_ref[0] = _dot_nt(wqc_ref[...], h).astype(BF16)
    vc_ref[0, 0] = (_dot_nt(wvc_ref[...], h) + ones_rows).astype(BF16)
    k_pairs = _dot(h, wkf_ref[...]).astype(BF16).astype(F32)
    q_rows = _dot_nt(wqf_ref[...], h)

    lf = jnp.minimum(z, 0.0) - jnp.log1p(jnp.exp(-jnp.abs(z)))
    row = lax.broadcasted_iota(jnp.int32, (tm, tm), 0)
    col = lax.broadcasted_iota(jnp.int32, (tm, tm), 1)
    tri = (col <= row).astype(BF16)
    l1 = lf.astype(BF16)
    l2 = (lf - l1.astype(F32)).astype(BF16)
    c = _dot(tri, l1) + _dot(tri, l2) + carry_ref[...]
    carry_ref[...] = c[tm - 1:tm, :]

    lane = lax.broadcasted_iota(jnp.int32, c.shape, 1)
    gate = jnp.where(lane == ONES_LANE, jnp.ones_like(c).astype(BF16), _split3_lanes(c * LOG2E))

    k_gate = _dot(gate, ek_ref[...])
    feature_lane = lax.broadcasted_iota(jnp.int32, (tm, HEAD_PAD), 1) < HEAD_DIM
    k_heads = []
    for hh in range(FOX_HEADS):
        group = slice((hh // 2) * HEAD_PAD, (hh // 2 + 1) * HEAD_PAD)
        pair, pair_gate = k_pairs[:, group], k_gate[:, group]
        if hh % 2:
            pair, pair_gate = pltpu.roll(pair, HEAD_DIM, 1), pltpu.roll(pair_gate, HEAD_DIM, 1)
        k_heads.append(jnp.where(feature_lane, pair, pair_gate))
    for hh in range(FOX_HEADS):
        kf_ref[0, hh, 0] = k_heads[hh].astype(BF16)
    qf = (q_rows + _dot_nt(eq_ref[...], gate)).astype(BF16)
    qf_ref[0] = qf

    qf32 = qf.astype(F32)
    k_norm2 = jnp.max(_dot((k_pairs * k_pairs).astype(BF16), gk_ref[...]), axis=0, keepdims=True)
    q_norm2_rows = _dot(gq_ref[...], (qf32 * qf32).astype(BF16))
    q_norm2 = jnp.max(q_norm2_rows, axis=1, keepdims=True)
    c2 = c * LOG2E
    srow = lax.broadcasted_iota(jnp.int32, ks_ref.shape[2:], 0)
    ks_ref[0, 0] = jnp.where(srow == 0, c2[tm - 1:tm, :], jnp.where(srow == 1, c2[0:1, :], k_norm2))
    qs_ref[0, 0] = jnp.broadcast_to(q_norm2, qs_ref.shape[2:])
    qn_ref[0] = q_norm2_rows
    cr_ref[0] = jnp.transpose(c2)[:qn_ref.shape[1], :]


def _inproj(x, g, wp):
    B, S, D = x.shape
    tm = TOKEN_TILE
    nk = S // tm
    hc = CHUNK_HEADS * HEAD_DIM
    hv = FOX_HEADS * V_ROWS
    tile = lambda b, i: (b, i, 0)
    out_shapes = (
        jax.ShapeDtypeStruct((B, FOX_HEADS, nk, tm, HEAD_PAD), BF16),
        jax.ShapeDtypeStruct((B, FOX_HEADS * Q_ROWS, S), BF16),
        jax.ShapeDtypeStruct((B, FOX_HEADS, nk, V_ROWS, tm), BF16),
        jax.ShapeDtypeStruct((B, nk, tm, hc), BF16),
        jax.ShapeDtypeStruct((B, hc, S), BF16),
        jax.ShapeDtypeStruct((B, nk, hv, tm), BF16),
        jax.ShapeDtypeStruct((B, nk, 8, LANES), F32),
        jax.ShapeDtypeStruct((B, nk, 2 * FOX_HEADS, LANES), F32),
        jax.ShapeDtypeStruct((B, 2 * FOX_HEADS, S), F32),
        jax.ShapeDtypeStruct((B, 2 * FOX_HEADS, S), F32),
    )
    stat_spec = lambda r: pl.BlockSpec((1, 1, r, LANES), lambda b, i: (b, i, 0, 0))
    k_spec = lambda w: pl.BlockSpec((1, 1, tm, w), lambda b, i: (b, i, 0, 0))
    q_spec = lambda w: pl.BlockSpec((1, w, tm), lambda b, i: (b, 0, i))
    v_spec = pl.BlockSpec((1, 1, hv, tm), lambda b, i: (b, i, 0, 0))
    head_major = lambda r, c: pl.BlockSpec((1, FOX_HEADS, 1, r, c), lambda b, i: (b, 0, i, 0, 0))
    weights = [g, wp["wf"], wp["bf"], wp["wkf"], wp["wqf"], wp["wvf"], wp["wkc"], wp["wqc"], wp["wvc"],
               wp["ek"], wp["eq"], wp["ones_rows"], wp["gk"], wp["gq"]]
    return pl.pallas_call(
        _inproj_kernel,
        grid=(B, nk),
        in_specs=[pl.BlockSpec((1, tm, D), tile)] + [_const_spec(w.shape) for w in weights],
        out_specs=(head_major(tm, HEAD_PAD), q_spec(FOX_HEADS * Q_ROWS), head_major(V_ROWS, tm),
                   k_spec(hc), q_spec(hc), v_spec,
                   stat_spec(8), stat_spec(2 * FOX_HEADS), q_spec(2 * FOX_HEADS), q_spec(2 * FOX_HEADS)),
        out_shape=out_shapes,
        scratch_shapes=[pltpu.VMEM((1, LANES), F32)],
        compiler_params=_params(("arbitrary", "arbitrary")),
        name="inproj",
    )(x, *weights)


def _fox_kernel(thr_ref, ck_ref, kn_ref, slow_ref, q_ref, qnext_ref, qn2_ref, crow_ref, k_ref, v_ref, o_ref,
                s0_ref, s1_ref, mx0_ref, mx1_ref, sf_ref, mxf_ref, acc_ref, m_ref):
    iq = pl.program_id(2)
    nq = pl.num_programs(2)
    tq = q_ref.shape[2]
    nkb, tk = k_ref.shape[1:3]
    tv = v_ref.shape[3]
    def contraction_rows(q):
        return jnp.concatenate([q, jnp.zeros((k_ref.shape[3] - q.shape[0], tq), q.dtype)], axis=0)

    qT = contraction_rows(q_ref[0])
    m_ref[...] = jnp.full(m_ref.shape, NEG_INF, F32)
    acc_ref[...] = jnp.zeros_like(acc_ref)
    cur = lax.bitwise_and(iq, 1)
    nxt = 1 - cur
    head = pl.program_id(0) * pl.num_programs(1) + pl.program_id(1)

    def first_block(i):
        thr = thr_ref[head * nq + i]
        count = jnp.int32(0)
        for j in range(nkb):
            skip = jnp.logical_and(j < i, thr < ck_ref[head * nkb + j])
            count = count + jnp.where(skip, 1, 0)
        return count

    j0 = first_block(iq)

    ckeys = tk // FOX_CHUNKS
    chunk_rows = [slice(c * ckeys, (c + 1) * ckeys) for c in range(FOX_CHUNKS)]

    def causal(c, s):
        key = lax.broadcasted_iota(jnp.int32, (ckeys, tq), 0) + c * ckeys
        qry = lax.broadcasted_iota(jnp.int32, (ckeys, tq), 1)
        return jnp.where(key <= qry, s, NEG_INF)

    def diagonal_max(s_ref):
        return functools.reduce(jnp.maximum, [jnp.max(causal(c, s_ref[rows, :]), axis=0, keepdims=True)
                                              for c, rows in enumerate(chunk_rows)])

    def step(jn, q_n, sn_ref, mxn_ref, jc=None, sc_ref=None, mx_c=None, diagonal=False):
        if jc is not None:
            m_old = m_ref[...]
            m_new = jnp.maximum(m_old, mx_c)
        mx_parts, pv = [], None
        for c, rows in enumerate(chunk_rows):
            if jc is not None:
                s_c = sc_ref[rows, :]
                p = jnp.exp2((causal(c, s_c) if diagonal else s_c) - m_new).astype(BF16)
            s = _dot(k_ref[0, jn, rows, :], q_n)
            sn_ref[rows, :] = s
            mx_parts.append(jnp.max(s, axis=0, keepdims=True))
            if jc is not None:
                off = (c * ckeys) % tv
                part = _dot(v_ref[0, jc * (tk // tv) + (c * ckeys) // tv, :, off:off + ckeys], p)
                pv = part if pv is None else pv + part
        mxn_ref[...] = functools.reduce(jnp.maximum, mx_parts)
        if jc is not None:
            acc_ref[...] = jnp.exp2(m_old - m_new) * acc_ref[...] + pv
            m_ref[...] = m_new

    def diagonal_and_handover(s_ref):
        step(first_block(jnp.minimum(iq + 1, nq - 1)), contraction_rows(qnext_ref[0]), sf_ref.at[nxt], mxf_ref.at[nxt],
             iq, s_ref, diagonal_max(s_ref), diagonal=True)

    def exact_max_path():
        @pl.when(iq == 0)
        def _():
            step(0, qT, sf_ref.at[cur], mxf_ref.at[cur])

        @pl.when(j0 == iq)
        def _():
            diagonal_and_handover(sf_ref.at[cur])

        @pl.when(j0 < iq)
        def _():
            step(j0 + 1, qT, s1_ref, mx1_ref, j0, sf_ref.at[cur], mxf_ref[cur])
            rest = iq - j0 - 1

            def pair(t, carry):
                j = j0 + 2 * t + 1
                step(j + 1, qT, s0_ref, mx0_ref, j, s1_ref, mx1_ref[...])
                step(j + 2, qT, s1_ref, mx1_ref, j + 1, s0_ref, mx0_ref[...])
                return carry

            lax.fori_loop(0, lax.shift_right_logical(rest, 1), pair, 0)
            odd = lax.bitwise_and(rest, 1)

            @pl.when(odd == 0)
            def _():
                diagonal_and_handover(s1_ref)

            @pl.when(odd == 1)
            def _():
                step(iq, qT, s0_ref, mx0_ref, iq - 1, s1_ref, mx1_ref[...])
                diagonal_and_handover(s0_ref)

    def bound_path():
        q_norm = jnp.sqrt(qn2_ref[0, 0]) * NORM_MARGIN
        c_t = crow_ref[0, 0]

        def pv_chunk(j, c, p):
            off = (c * ckeys) % tv
            return _dot(v_ref[0, j * (tk // tv) + (c * ckeys) // tv, :, off:off + ckeys], p)

        def run_blocks(blocks):
            m = m_ref[...]
            stabiliser = []
            for _, bound, _ in blocks:
                m_new = jnp.maximum(m, bound)
                stabiliser.append((m, m_new))
                m = m_new
            m_ref[...] = m
            stream = [(b, c) for b in range(len(blocks)) for c in range(FOX_CHUNKS)]
            weights, pv = {}, [None] * len(blocks)

            def finish(b, c):
                w, q0 = weights.pop((b, c))
                part = pv_chunk(blocks[b][0], c, w)
                if q0:
                    part = jnp.concatenate([jnp.zeros((part.shape[0], q0), F32), part], axis=1)
                pv[b] = part if pv[b] is None else pv[b] + part
                if c == FOX_CHUNKS - 1:
                    m_old, m_new = stabiliser[b]
                    acc_ref[...] = jnp.exp2(m_old - m_new) * acc_ref[...] + pv[b]

            for i, (b, c) in enumerate(stream):
                j, _, diagonal = blocks[b]
                q0 = c * ckeys if diagonal else 0
                s = _dot(k_ref[0, j, chunk_rows[c], :], qT[:, q0:])
                if diagonal:
                    key = lax.broadcasted_iota(jnp.int32, s.shape, 0)
                    qry = lax.broadcasted_iota(jnp.int32, s.shape, 1)
                    s = jnp.where(key <= qry, s, NEG_INF)
                weights[b, c] = jnp.exp2(s - stabiliser[b][1][:, q0:]).astype(BF16), q0
                if i >= PV_LAG:
                    finish(*stream[i - PV_LAG])
            for item in stream[-PV_LAG:]:
                finish(*item)

        def earlier(j):
            return j, q_norm * kn_ref[head * nkb + j] + (c_t - ck_ref[head * nkb + j]), False

        diagonal = (iq, q_norm * kn_ref[head * nkb + iq], True)

        def group(t, carry):
            run_blocks([earlier(j0 + FOX_GROUP * t + b) for b in range(FOX_GROUP)])
            return carry

        n_earlier = iq - j0
        n_groups = lax.shift_right_logical(n_earlier, FOX_GROUP.bit_length() - 1)
        lax.fori_loop(0, n_groups, group, 0)
        left = n_earlier - FOX_GROUP * n_groups

        for r in range(FOX_GROUP):
            @pl.when(left == r)
            def _(r=r):
                run_blocks([earlier(iq - r + b) for b in range(r)] + [diagonal])

    use_bound = slow_ref[head] == 0
    pl.when(use_bound)(bound_path)
    pl.when(jnp.logical_not(use_bound))(exact_max_path)

    acc = acc_ref[...]
    o_ref[0] = (acc[:HEAD_DIM] / acc[HEAD_DIM:HEAD_DIM + 1]).astype(o_ref.dtype)


def _fox_skip_tables(kstats, qstats):
    r = FOX_BLOCK // TOKEN_TILE
    B, nt = kstats.shape[:2]
    H = FOX_HEADS
    per_block = lambda a: a.reshape(B, nt // r, r, H)
    ck = per_block(kstats[:, :, 0, :H])[:, :, r - 1]
    cq = per_block(kstats[:, :, 1, :H])[:, :, 0]
    k_norm = jnp.sqrt(jnp.max(kstats[:, :, 2, :H], axis=1))
    q_norm = jnp.sqrt(jnp.max(per_block(qstats[:, :, :H, 0]), axis=2))
    gap = 2.0 * NORM_MARGIN * q_norm * k_norm[:, None, :]
    thr = cq + gap + SKIP_LOG2
    kn = jnp.sqrt(jnp.max(per_block(kstats[:, :, 2, :H]), axis=2)) * NORM_MARGIN
    slow = jnp.logical_not(jnp.max(gap, axis=1) < BOUND_LIMIT).astype(jnp.int32)
    flat = lambda a: jnp.transpose(a, (0, 2, 1)).reshape(-1).astype(F32)
    return flat(thr), flat(ck), flat(kn), slow.reshape(-1)


def _fox(kf, qf, vf, kstats, qstats, q_norm2, c_rows):
    B, H, nv, tv, _ = kf.shape
    S = qf.shape[2]
    tq = tk = FOX_BLOCK
    kf = kf.reshape(B * H, S // tk, tk, HEAD_PAD)
    vf = vf.reshape(B * H, nv, V_ROWS, tv)
    nq = S // tq
    tables = _fox_skip_tables(kstats, qstats)
    per_query = lambda a: a.reshape(B, a.shape[1], 1, S)
    row_spec = pl.BlockSpec((1, 1, 1, tq), lambda b, h, i, *_: (b, h, 0, i))
    grid_spec = pltpu.PrefetchScalarGridSpec(
        num_scalar_prefetch=len(tables),
        grid=(B, H, nq),
        in_specs=[
            pl.BlockSpec((1, Q_ROWS, tq), lambda b, h, i, *_: (b, h, i)),
            pl.BlockSpec((1, Q_ROWS, tq), lambda b, h, i, *_: (b, h, jnp.minimum(i + 1, nq - 1))),
            row_spec,
            row_spec,
            pl.BlockSpec((1, S // tk, tk, HEAD_PAD), lambda b, h, i, *_: (b * H + h, 0, 0, 0)),
            pl.BlockSpec((1, nv, V_ROWS, tv), lambda b, h, i, *_: (b * H + h, 0, 0, 0)),
        ],
        out_specs=pl.BlockSpec((1, HEAD_DIM, tq), lambda b, h, i, *_: (b, h, i)),
        scratch_shapes=[pltpu.VMEM((tk, tq), F32), pltpu.VMEM((tk, tq), F32),
                        pltpu.VMEM((1, tq), F32), pltpu.VMEM((1, tq), F32),
                        pltpu.VMEM((2, tk, tq), F32), pltpu.VMEM((2, 1, tq), F32),
                        pltpu.VMEM((V_ROWS, tq), F32), pltpu.VMEM((1, tq), F32)],
    )
    return pl.pallas_call(
        _fox_kernel,
        grid_spec=grid_spec,
        out_shape=jax.ShapeDtypeStruct((B, H * HEAD_DIM, S), BF16),
        compiler_params=_params(("arbitrary", "arbitrary", "arbitrary")),
        name="fox",
    )(*tables, qf, qf, per_query(q_norm2), per_query(c_rows), kf, vf)


BAND_BLOCKS = LEFT_CHUNKS * CHUNK // LANES


def _bias_block_id(delta):
    if delta < 0 or delta > BAND_BLOCKS:
        return None
    return {0: 0, 1: 1, BAND_BLOCKS: 3}.get(delta, 2)


def _chunk_kernel(q_ref, kp_ref, kc_ref, vp_ref, vc_ref, blk_ref, o_ref):
    iq = pl.program_id(1)
    tq = q_ref.shape[2]
    nb = tq // LANES
    before_start = jnp.where(iq > 0, 0.0, NEG_INF).astype(F32)
    zero_blk = jnp.zeros((LANES, LANES), BF16)
    row = lax.broadcasted_iota(jnp.int32, (2 * HEAD_DIM, tq), 0)

    def scores(h):
        pair, odd = divmod(h, 2)
        lanes = slice(pair * 2 * HEAD_DIM, (pair + 1) * 2 * HEAD_DIM)
        q_pair = q_ref[0, lanes, :]
        qT = jnp.where((row >= HEAD_DIM) == bool(odd), q_pair, jnp.zeros_like(q_pair))
        return _dot(kp_ref[0, 0, :, lanes], qT), _dot(kc_ref[0, 0, :, lanes], qT)

    ahead = [scores(0), scores(1), scores(2)]
    for h in range(CHUNK_HEADS):
        s_half = ahead.pop(0)
        if h + 3 < CHUNK_HEADS:
            ahead.append(scores(h + 3))
        p_blocks = {}
        for b in range(nb):
            cols = slice(b * LANES, (b + 1) * LANES)
            entries = []
            for half in (0, 1):
                for al in range(nb):
                    bid = _bias_block_id(b - (al - nb * (1 - half)))
                    if bid is None:
                        continue
                    sb = s_half[half][al * LANES:(al + 1) * LANES, cols] + blk_ref[h, bid]
                    if half == 0:
                        sb = sb + before_start
                    entries.append((half, al, sb))
            m = functools.reduce(jnp.maximum, [jnp.max(sb, axis=0, keepdims=True) for _, _, sb in entries])
            for half, al, sb in entries:
                p_blocks[half, al, b] = jnp.exp2(sb - m).astype(BF16)
        p_half = [jnp.concatenate([jnp.concatenate([p_blocks.get((half, al, b), zero_blk) for b in range(nb)],
                                                   axis=1) for al in range(nb)], axis=0) for half in (0, 1)]
        vrows = slice(h * V_ROWS, (h + 1) * V_ROWS)
        acc = _dot(vc_ref[0, 0, vrows, :], p_half[1]) + _dot(vp_ref[0, 0, vrows, :], p_half[0])
        o_ref[0, h * HEAD_DIM:(h + 1) * HEAD_DIM, :] = (
            acc[:HEAD_DIM] / acc[HEAD_DIM:HEAD_DIM + 1]).astype(o_ref.dtype)


def _chunk(kc, qc, vc, blocks):
    B, nk, tk, hc = kc.shape
    S = qc.shape[2]
    hv = vc.shape[2]
    tq = tk
    assert tq == LEFT_CHUNKS * CHUNK and 2 * CHUNK == LANES
    prev = lambda i: jnp.maximum(i - 1, 0)
    return pl.pallas_call(
        _chunk_kernel,
        grid=(B, S // tq),
        in_specs=[
            pl.BlockSpec((1, hc, tq), lambda b, i: (b, 0, i)),
            pl.BlockSpec((1, 1, tk, hc), lambda b, i: (b, prev(i), 0, 0)),
            pl.BlockSpec((1, 1, tk, hc), lambda b, i: (b, i, 0, 0)),
            pl.BlockSpec((1, 1, hv, tk), lambda b, i: (b, prev(i), 0, 0)),
            pl.BlockSpec((1, 1, hv, tk), lambda b, i: (b, i, 0, 0)),
            _const_spec(blocks.shape),
        ],
        out_specs=pl.BlockSpec((1, hc, tq), lambda b, i: (b, 0, i)),
        out_shape=jax.ShapeDtypeStruct((B, hc, S), BF16),
        compiler_params=_params(("arbitrary", "arbitrary")),
        name="chunk",
    )(qc, kc, kc, vc, vc, blocks)


def _chunk_bias_blocks(rel_bias):
    heads = rel_bias.shape[0]
    n = LANES
    span = 3 * n - 1
    dist = np.arange(span) - (n - 1)
    by_dist = rel_bias.astype(F32)[:, np.clip(dist, -(CHUNK - 1), REL_CLIP) + (CHUNK - 1)] * LOG2E
    padded = jnp.pad(by_dist, ((0, 0), (0, 1)))
    skew = jnp.broadcast_to(padded[:, None, :], (heads, n, span + 1)).reshape(heads, n * (span + 1))
    near = skew[:, :n * span].reshape(heads, n, span)[:, :, n - 1:]
    key_chunk = (np.arange(n) // CHUNK)[:, None]
    qry_chunk = (np.arange(n) // CHUNK)[None, :]
    clipped = jnp.broadcast_to(by_dist[:, -1][:, None, None], (heads, n, n))
    blocks = [
        jnp.where((qry_chunk >= key_chunk)[None], near[:, :, :n], NEG_INF),
        near[:, :, n:],
        clipped,
        jnp.where((2 * BAND_BLOCKS + qry_chunk - key_chunk <= LEFT_CHUNKS)[None], clipped, NEG_INF),
    ]
    return jnp.stack(blocks, axis=1)


def _memkv_kernel(mem_ref, g_ref, w_ref, kv_ref):
    m = _rms_rows(mem_ref[0], g_ref[...]).astype(BF16)
    kv_ref[0] = _dot(m, w_ref[...]).astype(BF16)


def _memkv(mem, g, w):
    B, N, D = mem.shape
    return pl.pallas_call(
        _memkv_kernel,
        grid=(B,),
        in_specs=[pl.BlockSpec((1, N, D), lambda b: (b, 0, 0)), _const_spec(g.shape), _const_spec(w.shape)],
        out_specs=pl.BlockSpec((1, N, 2 * D), lambda b: (b, 0, 0)),
        out_shape=jax.ShapeDtypeStruct((B, N, 2 * D), BF16),
        compiler_params=_params(("arbitrary",)),
        name="memkv",
    )(mem, g, w)


def _mix_mem_kernel(x_ref, of_ref, oc_ref, kv_ref, gfo_ref, gco_ref, wout_ref, gmp_ref,
                    gmq_ref, wq_ref, wo_ref, gmo_ref, y_ref):
    tm, d = x_ref.shape[1:]
    dh = d // MEM_HEADS
    kv = kv_ref[0]
    groups = [slice(r * tm // MIX_GROUPS, (r + 1) * tm // MIX_GROUPS) for r in range(MIX_GROUPS)]

    def mixer_input(rows):
        of = jnp.transpose(of_ref[0, :, rows].astype(F32))
        oc = jnp.transpose(oc_ref[0, :, rows].astype(F32))
        return jnp.concatenate([_rms_rows(of, gfo_ref[...]), _rms_rows(oc, gco_ref[...])], axis=-1).astype(BF16)

    def cross_attention(q):
        outs = []
        for hh in range(MEM_HEADS):
            sl = slice(hh * dh, (hh + 1) * dh)
            s = _dot_nt(q[:, sl], kv[:, sl])
            p = jnp.exp2(s - jnp.max(s, axis=-1, keepdims=True))
            l = jnp.sum(p, axis=-1, keepdims=True)
            outs.append(_dot(p.astype(BF16), kv[:, d + hh * dh:d + (hh + 1) * dh]) / l)
        return jnp.concatenate(outs, axis=-1).astype(BF16)

    cat = [mixer_input(rows) for rows in groups]
    x1 = [x_ref[0, rows, :] + _rms_rows(_dot(c, wout_ref[...]), gmp_ref[...]) for rows, c in zip(groups, cat)]
    q = [_dot(_rms_rows(v, gmq_ref[...]).astype(BF16), wq_ref[...]).astype(BF16) for v in x1]
    ca = [cross_attention(v) for v in q]
    for rows, v, c in zip(groups, x1, ca):
        y_ref[0, rows, :] = v + _rms_rows(_dot(c, wo_ref[...]), gmo_ref[...])


def _mix_mem(x, of, oc, kv, wp):
    B, S, D = x.shape
    tm = MIX_GROUPS * TOKEN_TILE
    half = of.shape[1]
    weights = [wp["g_fox_out"], wp["g_chunk_out"], wp["w_out"], wp["g_mix_post"],
               wp["g_mem_pre"], wp["w_mem_q"], wp["w_mem_o"], wp["g_mem_post"]]
    return pl.pallas_call(
        _mix_mem_kernel,
        grid=(B, S // tm),
        in_specs=[
            pl.BlockSpec((1, tm, D), lambda b, i: (b, i, 0)),
            pl.BlockSpec((1, half, tm), lambda b, i: (b, 0, i)),
            pl.BlockSpec((1, half, tm), lambda b, i: (b, 0, i)),
            pl.BlockSpec((1,) + kv.shape[1:], lambda b, i: (b, 0, 0)),
        ] + [_const_spec(w.shape) for w in weights],
        out_specs=pl.BlockSpec((1, tm, D), lambda b, i: (b, i, 0)),
        out_shape=jax.ShapeDtypeStruct((B, S, D), F32),
        compiler_params=_params(("arbitrary", "arbitrary")),
        name="mix_mem",
    )(x, of, oc, kv, *weights)


FFN_CHUNK = 1024
FFN_GROUPS = 2


def _ffn_kernel(x_ref, gpre_ref, wg_ref, wu_ref, wd_ref, gpost_ref, y_ref, act_ref):
    tm = x_ref.shape[1]
    dff = wg_ref.shape[1]
    groups = [slice(r * tm // FFN_GROUPS, (r + 1) * tm // FFN_GROUPS) for r in range(FFN_GROUPS)]
    h = [_rms_rows(x_ref[0, rows, :], gpre_ref[...]).astype(BF16) for rows in groups]
    for rows, hr in zip(groups, h):
        for c0 in range(0, dff, FFN_CHUNK):
            c1 = min(c0 + FFN_CHUNK, dff)
            g = _dot(hr, wg_ref[:, c0:c1])
            u = _dot(hr, wu_ref[:, c0:c1])
            act_ref[rows, c0:c1] = (jax.nn.silu(g) * u).astype(BF16)
    down = [_dot(act_ref[rows, :], wd_ref[...]) for rows in groups]
    for rows, dn in zip(groups, down):
        y_ref[0, rows, :] = x_ref[0, rows, :] + _rms_rows(dn, gpost_ref[...])


def _ffn(x, wp):
    B, S, D = x.shape
    tm = FFN_GROUPS * TOKEN_TILE
    weights = [wp["g_ffn_pre"], wp["w_gate"], wp["w_up"], wp["w_down"], wp["g_ffn_post"]]
    dff = wp["w_gate"].shape[1]
    return pl.pallas_call(
        _ffn_kernel,
        grid=(B, S // tm),
        in_specs=[pl.BlockSpec((1, tm, D), lambda b, i: (b, i, 0))] + [_const_spec(w.shape) for w in weights],
        out_specs=pl.BlockSpec((1, tm, D), lambda b, i: (b, i, 0)),
        out_shape=jax.ShapeDtypeStruct((B, S, D), F32),
        scratch_shapes=[pltpu.VMEM((tm, dff), BF16)],
        compiler_params=_params(("arbitrary", "arbitrary")),
        name="ffn",
    )(x, *weights)


def _pad_heads_cols(w, heads, width):
    d = w.shape[0]
    w = w.reshape(d, heads, HEAD_DIM)
    w = jnp.pad(w, ((0, 0), (0, 0), (0, width - HEAD_DIM)))
    return w.reshape(d, heads * width)


def _placement(entries, rows, cols):
    m = np.zeros((rows, cols), np.float32)
    for r, c, v in entries:
        m[r, c] = v
    return jnp.asarray(m, BF16)


def _prep_layer(l, g_mix_pre, w_in, b_fgate, g_fox_out, g_chunk_out, w_out, g_mix_post, g_mem_pre, g_mem_kv,
                w_mem_q, w_mem_kv, w_mem_o, g_mem_post, g_ffn_pre, w_gate_up, w_down, g_ffn_post):
    d = w_in.shape[1]
    fw = FOX_HEADS * HEAD_DIM
    cw = CHUNK_HEADS * HEAD_DIM
    w = w_in[l]
    o = 0
    wq_f, wk_f, wv_f = w[:, o:o + fw], w[:, o + fw:o + 2 * fw], w[:, o + 2 * fw:o + 3 * fw]
    o += 3 * fw
    w_f = w[:, o:o + FOX_HEADS]
    o += FOX_HEADS
    wq_c, wk_c, wv_c = w[:, o:o + cw], w[:, o + cw:o + 2 * cw], w[:, o + 2 * cw:o + 3 * cw]
    scale = HEAD_DIM ** -0.5 * LOG2E

    row = lambda v: v.astype(F32)[None, :]
    wf3 = jnp.pad(jnp.tile(w_f, (1, N_SPLIT)), ((0, 0), (0, LANES - N_SPLIT * FOX_HEADS)))
    bf3 = jnp.pad(jnp.tile(b_fgate[l].astype(F32), N_SPLIT), (0, LANES - N_SPLIT * FOX_HEADS))[None, :]

    ek, eq = [], []
    for h in range(FOX_HEADS):
        for i in range(N_SPLIT):
            gate_lane = (h // 2) * HEAD_PAD + (HEAD_DIM if h % 2 == 0 else 0)
            ek.append((ONES_LANE, gate_lane + i, 1.0))
            ek.append((i * FOX_HEADS + h, gate_lane + N_SPLIT + i, -1.0))
            eq.append((h * Q_ROWS + HEAD_DIM + i, i * FOX_HEADS + h, 1.0))
            eq.append((h * Q_ROWS + HEAD_DIM + N_SPLIT + i, ONES_LANE, 1.0))
    gk = [(h * HEAD_DIM + d, h, 1.0) for h in range(FOX_HEADS) for d in range(HEAD_DIM)]
    gq = [(h, h * Q_ROWS + d, 1.0) for h in range(FOX_HEADS) for d in range(HEAD_DIM)]
    ones_rows = np.zeros((FOX_HEADS * V_ROWS, LANES), np.float32)
    ones_rows[np.arange(FOX_HEADS) * V_ROWS + HEAD_DIM] = 1.0

    dff = w_gate_up.shape[2] // 2
    mem_scale = (w_mem_q.shape[2] // MEM_HEADS) ** -0.5 * LOG2E
    return dict(
        g_mix_pre=row(g_mix_pre[l]),
        wf=wf3.astype(BF16), bf=bf3,
        wkf=wk_f.astype(BF16),
        wqf=_pad_heads_cols(wq_f * scale, FOX_HEADS, Q_ROWS).T.astype(BF16),
        wvf=_pad_heads_cols(wv_f, FOX_HEADS, V_ROWS).T.astype(BF16),
        wkc=wk_c.astype(BF16),
        wqc=(wq_c * scale).T.astype(BF16),
        wvc=_pad_heads_cols(wv_c, CHUNK_HEADS, V_ROWS).T.astype(BF16),
        ek=_placement(ek, LANES, FOX_HEADS * HEAD_DIM),
        eq=_placement(eq, FOX_HEADS * Q_ROWS, LANES),
        ones_rows=jnp.asarray(ones_rows),
        gk=_placement(gk, FOX_HEADS * HEAD_DIM, LANES),
        gq=_placement(gq, 2 * FOX_HEADS, FOX_HEADS * Q_ROWS),
        g_fox_out=row(g_fox_out[l]), g_chunk_out=row(g_chunk_out[l]),
        w_out=w_out[l].astype(BF16), g_mix_post=row(g_mix_post[l]),
        g_mem_pre=row(g_mem_pre[l]), g_mem_kv=row(g_mem_kv[l]),
        w_mem_q=(w_mem_q[l] * mem_scale).astype(BF16), w_mem_kv=w_mem_kv[l].astype(BF16),
        w_mem_o=w_mem_o[l].astype(BF16), g_mem_post=row(g_mem_post[l]),
        g_ffn_pre=row(g_ffn_pre[l]),
        w_gate=w_gate_up[l][:, :dff].astype(BF16), w_up=w_gate_up[l][:, dff:].astype(BF16),
        w_down=w_down[l].astype(BF16), g_ffn_post=row(g_ffn_post[l]),
    )


def kernel(x, mem, g_mix_pre, w_in, b_fgate, rel_bias, g_fox_out, g_chunk_out, w_out, g_mix_post, g_mem_pre,
           g_mem_kv, w_mem_q, w_mem_kv, w_mem_o, g_mem_post, g_ffn_pre, w_gate_up, w_down, g_ffn_post):
    depth = w_in.shape[0]
    for l in range(depth):
        wp = _prep_layer(l, g_mix_pre, w_in, b_fgate, g_fox_out, g_chunk_out, w_out, g_mix_post, g_mem_pre,
                         g_mem_kv, w_mem_q, w_mem_kv, w_mem_o, g_mem_post, g_ffn_pre, w_gate_up, w_down,
                         g_ffn_post)
        kf, qf, vf, kc, qc, vc, kstats, qstats, q_norm2, c_rows = _inproj(x, wp["g_mix_pre"], wp)
        o_f = _fox(kf, qf, vf, kstats, qstats, q_norm2, c_rows)
        o_c = _chunk(kc, qc, vc, _chunk_bias_blocks(rel_bias[l]))
        kv = _memkv(mem, wp["g_mem_kv"], wp["w_mem_kv"])
        x = _mix_mem(x, o_f, o_c, kv, wp)
        x = _ffn(x, wp)
    return x
```

```python
import functools

import jax
import jax.numpy as jnp
import numpy as np
from jax import lax
from jax.experimental import pallas as pl
from jax.experimental.pallas import tpu as pltpu

F32 = jnp.float32
BF16 = jnp.bfloat16

HEAD_DIM = 64
FOX_HEADS = 8
CHUNK_HEADS = 8
CHUNK = 64
LEFT_CHUNKS = 8
REL_CLIP = 128
MEM_HEADS = 4
EPS = 1e-6
NEG_INF = -1e30

LANES = 128
HEAD_PAD = 128
V_ROWS = 80
Q_ROWS = 80
ONES_LANE = LANES - 1
N_SPLIT = 3
TOKEN_TILE = 512
FOX_BLOCK = 1024
FOX_CHUNKS = 4
FOX_GROUP = 4
PV_LAG = 2
MIX_GROUPS = 2
LOG2E = 1.4426950408889634
SKIP_LOG2 = 150.0
NORM_MARGIN = 1.02
BOUND_LIMIT = 100.0
VMEM_LIMIT = 56 * 1024 * 1024


def _params(semantics):
    return pltpu.CompilerParams(dimension_semantics=semantics, vmem_limit_bytes=VMEM_LIMIT)


def _const_spec(shape):
    nd = len(shape)
    return pl.BlockSpec(shape, lambda *_: (0,) * nd, pipeline_mode=pl.Buffered(1))


def _rms_rows(x, g):
    ms = jnp.mean(x * x, axis=-1, keepdims=True)
    return x * lax.rsqrt(ms + EPS) * g


def _split3_lanes(v):
    lane = lax.broadcasted_iota(jnp.int32, v.shape, 1)
    t1 = v.astype(BF16)
    r1 = v - t1.astype(F32)
    t2 = r1.astype(BF16)
    r2 = r1 - t2.astype(F32)
    t3 = r2.astype(BF16)
    return jnp.where(lane < FOX_HEADS, t1, jnp.where(lane < 2 * FOX_HEADS, t2, t3))


def _dot(a, b):
    return jnp.dot(a, b, preferred_element_type=F32)


def _dot_nt(a, b):
    return lax.dot_general(a, b, (((1,), (1,)), ((), ())), preferred_element_type=F32)


def _inproj_kernel(x_ref, g_ref, wf_ref, bf_ref, wkf_ref, wqf_ref, wvf_ref, wkc_ref, wqc_ref, wvc_ref,
                   ek_ref, eq_ref, ones_ref, gk_ref, gq_ref,
                   kf_ref, qf_ref, vf_ref, kc_ref, qc_ref, vc_ref, ks_ref, qs_ref, qn_ref, cr_ref, carry_ref):
    tm = x_ref.shape[1]

    @pl.when(pl.program_id(1) == 0)
    def _():
        carry_ref[...] = jnp.zeros_like(carry_ref)

    h = _rms_rows(x_ref[0], g_ref[...]).astype(BF16)

    z = _dot(h, wf_ref[...]) + bf_ref[...]

    ones_rows = jnp.tile(ones_ref[...], (1, tm // LANES))
    vf = (_dot_nt(wvf_ref[...], h) + ones_rows).astype(BF16)
    for hh in range(FOX_HEADS):
        vf_ref[0, hh, 0] = vf[hh * V_ROWS:(hh + 1) * V_ROWS]
    kc_ref[0, 0] = _dot(h, wkc_ref[...]).astype(BF16)
    qc_ref[0] = _dot_nt(wqc_ref[...], h).astype(BF16)
    vc_ref[0, 0] = (_dot_nt(wvc_ref[...], h) + ones_rows).astype(BF16)
    k_pairs = _dot(h, wkf_ref[...]).astype(BF16).astype(F32)
    q_rows = _dot_nt(wqf_ref[...], h)

    lf = jnp.minimum(z, 0.0) - jnp.log1p(jnp.exp(-jnp.abs(z)))
    row = lax.broadcasted_iota(jnp.int32, (tm, tm), 0)
    col = lax.broadcasted_iota(jnp.int32, (tm, tm), 1)
    tri = (col <= row).astype(BF16)
    l1 = lf.astype(BF16)
    l2 = (lf - l1.astype(F32)).astype(BF16)
    c = _dot(tri, l1) + _dot(tri, l2) + carry_ref[...]
    carry_ref[...] = c[tm - 1:tm, :]

    lane = lax.broadcasted_iota(jnp.int32, c.shape, 1)
    gate = jnp.where(lane == ONES_LANE, jnp.ones_like(c).astype(BF16), _split3_lanes(c * LOG2E))

    k_gate = _dot(gate, ek_ref[...])
    feature_lane = lax.broadcasted_iota(jnp.int32, (tm, HEAD_PAD), 1) < HEAD_DIM
    k_heads = []
    for hh in range(FOX_HEADS):
        group = slice((hh // 2) * HEAD_PAD, (hh // 2 + 1) * HEAD_PAD)
        pair, pair_gate = k_pairs[:, group], k_gate[:, group]
        if hh % 2:
            pair, pair_gate = pltpu.roll(pair, HEAD_DIM, 1), pltpu.roll(pair_gate, HEAD_DIM, 1)
        k_heads.append(jnp.where(feature_lane, pair, pair_gate))
    for hh in range(FOX_HEADS):
        kf_ref[0, hh, 0] = k_heads[hh].astype(BF16)
    qf = (q_rows + _dot_nt(eq_ref[...], gate)).astype(BF16)
    qf_ref[0] = qf

    qf32 = qf.astype(F32)
    k_norm2 = jnp.max(_dot((k_pairs * k_pairs).astype(BF16), gk_ref[...]), axis=0, keepdims=True)
    q_norm2_rows = _dot(gq_ref[...], (qf32 * qf32).astype(BF16))
    q_norm2 = jnp.max(q_norm2_rows, axis=1, keepdims=True)
    c2 = c * LOG2E
    srow = lax.broadcasted_iota(jnp.int32, ks_ref.shape[2:], 0)
    ks_ref[0, 0] = jnp.where(srow == 0, c2[tm - 1:tm, :], jnp.where(srow == 1, c2[0:1, :], k_norm2))
    qs_ref[0, 0] = jnp.broadcast_to(q_norm2, qs_ref.shape[2:])
    qn_ref[0] = q_norm2_rows
    cr_ref[0] = jnp.transpose(c2)[:qn_ref.shape[1], :]


def _inproj(x, g, wp):
    B, S, D = x.shape
    tm = TOKEN_TILE
    nk = S // tm
    hc = CHUNK_HEADS * HEAD_DIM
    hv = FOX_HEADS * V_ROWS
    tile = lambda b, i: (b, i, 0)
    out_shapes = (
        jax.ShapeDtypeStruct((B, FOX_HEADS, nk, tm, HEAD_PAD), BF16),
        jax.ShapeDtypeStruct((B, FOX_HEADS * Q_ROWS, S), BF16),
        jax.ShapeDtypeStruct((B, FOX_HEADS, nk, V_ROWS, tm), BF16),
        jax.ShapeDtypeStruct((B, nk, tm, hc), BF16),
        jax.ShapeDtypeStruct((B, hc, S), BF16),
        jax.ShapeDtypeStruct((B, nk, hv, tm), BF16),
        jax.ShapeDtypeStruct((B, nk, 8, LANES), F32),
        jax.ShapeDtypeStruct((B, nk, 2 * FOX_HEADS, LANES), F32),
        jax.ShapeDtypeStruct((B, 2 * FOX_HEADS, S), F32),
        jax.ShapeDtypeStruct((B, 2 * FOX_HEADS, S), F32),
    )
    stat_spec = lambda r: pl.BlockSpec((1, 1, r, LANES), lambda b, i: (b, i, 0, 0))
    k_spec = lambda w: pl.BlockSpec((1, 1, tm, w), lambda b, i: (b, i, 0, 0))
    q_spec = lambda w: pl.BlockSpec((1, w, tm), lambda b, i: (b, 0, i))
    v_spec = pl.BlockSpec((1, 1, hv, tm), lambda b, i: (b, i, 0, 0))
    head_major = lambda r, c: pl.BlockSpec((1, FOX_HEADS, 1, r, c), lambda b, i: (b, 0, i, 0, 0))
    weights = [g, wp["wf"], wp["bf"], wp["wkf"], wp["wqf"], wp["wvf"], wp["wkc"], wp["wqc"], wp["wvc"],
               wp["ek"], wp["eq"], wp["ones_rows"], wp["gk"], wp["gq"]]
    return pl.pallas_call(
        _inproj_kernel,
        grid=(B, nk),
        in_specs=[pl.BlockSpec((1, tm, D), tile)] + [_const_spec(w.shape) for w in weights],
        out_specs=(head_major(tm, HEAD_PAD), q_spec(FOX_HEADS * Q_ROWS), head_major(V_ROWS, tm),
                   k_spec(hc), q_spec(hc), v_spec,
                   stat_spec(8), stat_spec(2 * FOX_HEADS), q_spec(2 * FOX_HEADS), q_spec(2 * FOX_HEADS)),
        out_shape=out_shapes,
        scratch_shapes=[pltpu.VMEM((1, LANES), F32)],
        compiler_params=_params(("arbitrary", "arbitrary")),
        name="inproj",
    )(x, *weights)


def _fox_kernel(thr_ref, ck_ref, kn_ref, slow_ref, q_ref, qnext_ref, qn2_ref, crow_ref, k_ref, v_ref, o_ref,
                s0_ref, s1_ref, mx0_ref, mx1_ref, sf_ref, mxf_ref, acc_ref, m_ref):
    iq = pl.program_id(2)
    nq = pl.num_programs(2)
    tq = q_ref.shape[2]
    nkb, tk = k_ref.shape[1:3]
    tv = v_ref.shape[3]
    def contraction_rows(q):
        return jnp.concatenate([q, jnp.zeros((k_ref.shape[3] - q.shape[0], tq), q.dtype)], axis=0)

    qT = contraction_rows(q_ref[0])
    m_ref[...] = jnp.full(m_ref.shape, NEG_INF, F32)
    acc_ref[...] = jnp.zeros_like(acc_ref)
    cur = lax.bitwise_and(iq, 1)
    nxt = 1 - cur
    head = pl.program_id(0) * pl.num_programs(1) + pl.program_id(1)

    def first_block(i):
        thr = thr_ref[head * nq + i]
        count = jnp.int32(0)
        for j in range(nkb):
            skip = jnp.logical_and(j < i, thr < ck_ref[head * nkb + j])
            count = count + jnp.where(skip, 1, 0)
        return count

    j0 = first_block(iq)

    ckeys = tk // FOX_CHUNKS
    chunk_rows = [slice(c * ckeys, (c + 1) * ckeys) for c in range(FOX_CHUNKS)]

    def causal(c, s):
        key = lax.broadcasted_iota(jnp.int32, (ckeys, tq), 0) + c * ckeys
        qry = lax.broadcasted_iota(jnp.int32, (ckeys, tq), 1)
        return jnp.where(key <= qry, s, NEG_INF)

    def diagonal_max(s_ref):
        return functools.reduce(jnp.maximum, [jnp.max(causal(c, s_ref[rows, :]), axis=0, keepdims=True)
                                              for c, rows in enumerate(chunk_rows)])

    def step(jn, q_n, sn_ref, mxn_ref, jc=None, sc_ref=None, mx_c=None, diagonal=False):
        if jc is not None:
            m_old = m_ref[...]
            m_new = jnp.maximum(m_old, mx_c)
        mx_parts, pv = [], None
        for c, rows in enumerate(chunk_rows):
            if jc is not None:
                s_c = sc_ref[rows, :]
                p = jnp.exp2((causal(c, s_c) if diagonal else s_c) - m_new).astype(BF16)
            s = _dot(k_ref[0, jn, rows, :], q_n)
            sn_ref[rows, :] = s
            mx_parts.append(jnp.max(s, axis=0, keepdims=True))
            if jc is not None:
                off = (c * ckeys) % tv
                part = _dot(v_ref[0, jc * (tk // tv) + (c * ckeys) // tv, :, off:off + ckeys], p)
                pv = part if pv is None else pv + part
        mxn_ref[...] = functools.reduce(jnp.maximum, mx_parts)
        if jc is not None:
            acc_ref[...] = jnp.exp2(m_old - m_new) * acc_ref[...] + pv
            m_ref[...] = m_new

    def diagonal_and_handover(s_ref):
        step(first_block(jnp.minimum(iq + 1, nq - 1)), contraction_rows(qnext_ref[0]), sf_ref.at[nxt], mxf_ref.at[nxt],
             iq, s_ref, diagonal_max(s_ref), diagonal=True)

    def exact_max_path():
        @pl.when(iq == 0)
        def _():
            step(0, qT, sf_ref.at[cur], mxf_ref.at[cur])

        @pl.when(j0 == iq)
        def _():
            diagonal_and_handover(sf_ref.at[cur])

        @pl.when(j0 < iq)
        def _():
            step(j0 + 1, qT, s1_ref, mx1_ref, j0, sf_ref.at[cur], mxf_ref[cur])
            rest = iq - j0 - 1

            def pair(t, carry):
                j = j0 + 2 * t + 1
                step(j + 1, qT, s0_ref, mx0_ref, j, s1_ref, mx1_ref[...])
                step(j + 2, qT, s1_ref, mx1_ref, j + 1, s0_ref, mx0_ref[...])
                return carry

            lax.fori_loop(0, lax.shift_right_logical(rest, 1), pair, 0)
            odd = lax.bitwise_and(rest, 1)

            @pl.when(odd == 0)
            def _():
                diagonal_and_handover(s1_ref)

            @pl.when(odd == 1)
            def _():
                step(iq, qT, s0_ref, mx0_ref, iq - 1, s1_ref, mx1_ref[...])
                diagonal_and_handover(s0_ref)

    def bound_path():
        q_norm = jnp.sqrt(qn2_ref[0, 0]) * NORM_MARGIN
        c_t = crow_ref[0, 0]

        def pv_chunk(j, c, p):
            off = (c * ckeys) % tv
            return _dot(v_ref[0, j * (tk // tv) + (c * ckeys) // tv, :, off:off + ckeys], p)

        def run_blocks(blocks):
            m = m_ref[...]
            stabiliser = []
            for _, bound, _ in blocks:
                m_new = jnp.maximum(m, bound)
                stabiliser.append((m, m_new))
                m = m_new
            m_ref[...] = m
            stream = [(b, c) for b in range(len(blocks)) for c in range(FOX_CHUNKS)]
            weights, pv = {}, [None] * len(blocks)

            def finish(b, c):
                w, q0 = weights.pop((b, c))
                part = pv_chunk(blocks[b][0], c, w)
                if q0:
                    part = jnp.concatenate([jnp.zeros((part.shape[0], q0), F32), part], axis=1)
                pv[b] = part if pv[b] is None else pv[b] + part
                if c == FOX_CHUNKS - 1:
                    m_old, m_new = stabiliser[b]
                    acc_ref[...] = jnp.exp2(m_old - m_new) * acc_ref[...] + pv[b]

            for i, (b, c) in enumerate(stream):
                j, _, diagonal = blocks[b]
                q0 = c * ckeys if diagonal else 0
                s = _dot(k_ref[0, j, chunk_rows[c], :], qT[:, q0:])
                if diagonal:
                    key = lax.broadcasted_iota(jnp.int32, s.shape, 0)
                    qry = lax.broadcasted_iota(jnp.int32, s.shape, 1)
                    s = jnp.where(key <= qry, s, NEG_INF)
                weights[b, c] = jnp.exp2(s - stabiliser[b][1][:, q0:]).astype(BF16), q0
                if i >= PV_LAG:
                    finish(*stream[i - PV_LAG])
            for item in stream[-PV_LAG:]:
                finish(*item)

        def earlier(j):
            return j, q_norm * kn_ref[head * nkb + j] + (c_t - ck_ref[head * nkb + j]), False

        diagonal = (iq, q_norm * kn_ref[head * nkb + iq], True)

        def group(t, carry):
            run_blocks([earlier(j0 + FOX_GROUP * t + b) for b in range(FOX_GROUP)])
            return carry

        n_earlier = iq - j0
        n_groups = lax.shift_right_logical(n_earlier, FOX_GROUP.bit_length() - 1)
        lax.fori_loop(0, n_groups, group, 0)
        left = n_earlier - FOX_GROUP * n_groups

        for r in range(FOX_GROUP):
            @pl.when(left == r)
            def _(r=r):
                run_blocks([earlier(iq - r + b) for b in range(r)] + [diagonal])

    use_bound = slow_ref[head] == 0
    pl.when(use_bound)(bound_path)
    pl.when(jnp.logical_not(use_bound))(exact_max_path)

    acc = acc_ref[...]
    o_ref[0] = (acc[:HEAD_DIM] / acc[HEAD_DIM:HEAD_DIM + 1]).astype(o_ref.dtype)


def _fox_skip_tables(kstats, qstats):
    r = FOX_BLOCK // TOKEN_TILE
    B, nt = kstats.shape[:2]
    H = FOX_HEADS
    per_block = lambda a: a.reshape(B, nt // r, r, H)
    ck = per_block(kstats[:, :, 0, :H])[:, :, r - 1]
    cq = per_block(kstats[:, :, 1, :H])[:, :, 0]
    k_norm = jnp.sqrt(jnp.max(kstats[:, :, 2, :H], axis=1))
    q_norm = jnp.sqrt(jnp.max(per_block(qstats[:, :, :H, 0]), axis=2))
    gap = 2.0 * NORM_MARGIN * q_norm * k_norm[:, None, :]
    thr = cq + gap + SKIP_LOG2
    kn = jnp.sqrt(jnp.max(per_block(kstats[:, :, 2, :H]), axis=2)) * NORM_MARGIN
    slow = jnp.logical_not(jnp.max(gap, axis=1) < BOUND_LIMIT).astype(jnp.int32)
    flat = lambda a: jnp.transpose(a, (0, 2, 1)).reshape(-1).astype(F32)
    return flat(thr), flat(ck), flat(kn), slow.reshape(-1)


def _fox(kf, qf, vf, kstats, qstats, q_norm2, c_rows):
    B, H, nv, tv, _ = kf.shape
    S = qf.shape[2]
    tq = tk = FOX_BLOCK
    kf = kf.reshape(B * H, S // tk, tk, HEAD_PAD)
    vf = vf.reshape(B * H, nv, V_ROWS, tv)
    nq = S // tq
    tables = _fox_skip_tables(kstats, qstats)
    per_query = lambda a: a.reshape(B, a.shape[1], 1, S)
    row_spec = pl.BlockSpec((1, 1, 1, tq), lambda b, h, i, *_: (b, h, 0, i))
    grid_spec = pltpu.PrefetchScalarGridSpec(
        num_scalar_prefetch=len(tables),
        grid=(B, H, nq),
        in_specs=[
            pl.BlockSpec((1, Q_ROWS, tq), lambda b, h, i, *_: (b, h, i)),
            pl.BlockSpec((1, Q_ROWS, tq), lambda b, h, i, *_: (b, h, jnp.minimum(i + 1, nq - 1))),
            row_spec,
            row_spec,
            pl.BlockSpec((1, S // tk, tk, HEAD_PAD), lambda b, h, i, *_: (b * H + h, 0, 0, 0)),
            pl.BlockSpec((1, nv, V_ROWS, tv), lambda b, h, i, *_: (b * H + h, 0, 0, 0)),
        ],
        out_specs=pl.BlockSpec((1, HEAD_DIM, tq), lambda b, h, i, *_: (b, h, i)),
        scratch_shapes=[pltpu.VMEM((tk, tq), F32), pltpu.VMEM((tk, tq), F32),
                        pltpu.VMEM((1, tq), F32), pltpu.VMEM((1, tq), F32),
                        pltpu.VMEM((2, tk, tq), F32), pltpu.VMEM((2, 1, tq), F32),
                        pltpu.VMEM((V_ROWS, tq), F32), pltpu.VMEM((1, tq), F32)],
    )
    return pl.pallas_call(
        _fox_kernel,
        grid_spec=grid_spec,
        out_shape=jax.ShapeDtypeStruct((B, H * HEAD_DIM, S), BF16),
        compiler_params=_params(("arbitrary", "arbitrary", "arbitrary")),
        name="fox",
    )(*tables, qf, qf, per_query(q_norm2), per_query(c_rows), kf, vf)


BAND_BLOCKS = LEFT_CHUNKS * CHUNK // LANES


def _bias_block_id(delta):
    if delta < 0 or delta > BAND_BLOCKS:
        return None
    return {0: 0, 1: 1, BAND_BLOCKS: 3}.get(delta, 2)


def _chunk_kernel(q_ref, kp_ref, kc_ref, vp_ref, vc_ref, blk_ref, o_ref):
    iq = pl.program_id(1)
    tq = q_ref.shape[2]
    nb = tq // LANES
    before_start = jnp.where(iq > 0, 0.0, NEG_INF).astype(F32)
    zero_blk = jnp.zeros((LANES, LANES), BF16)
    row = lax.broadcasted_iota(jnp.int32, (2 * HEAD_DIM, tq), 0)

    def scores(h):
        pair, odd = divmod(h, 2)
        lanes = slice(pair * 2 * HEAD_DIM, (pair + 1) * 2 * HEAD_DIM)
        q_pair = q_ref[0, lanes, :]
        qT = jnp.where((row >= HEAD_DIM) == bool(odd), q_pair, jnp.zeros_like(q_pair))
        return _dot(kp_ref[0, 0, :, lanes], qT), _dot(kc_ref[0, 0, :, lanes], qT)

    ahead = [scores(0), scores(1), scores(2)]
    for h in range(CHUNK_HEADS):
        s_half = ahead.pop(0)
        if h + 3 < CHUNK_HEADS:
            ahead.append(scores(h + 3))
        p_blocks = {}
        for b in range(nb):
            cols = slice(b * LANES, (b + 1) * LANES)
            entries = []
            for half in (0, 1):
                for al in range(nb):
                    bid = _bias_block_id(b - (al - nb * (1 - half)))
                    if bid is None:
                        continue
                    sb = s_half[half][al * LANES:(al + 1) * LANES, cols] + blk_ref[h, bid]
                    if half == 0:
                        sb = sb + before_start
                    entries.append((half, al, sb))
            m = functools.reduce(jnp.maximum, [jnp.max(sb, axis=0, keepdims=True) for _, _, sb in entries])
            for half, al, sb in entries:
                p_blocks[half, al, b] = jnp.exp2(sb - m).astype(BF16)
        p_half = [jnp.concatenate([jnp.concatenate([p_blocks.get((half, al, b), zero_blk) for b in range(nb)],
                                                   axis=1) for al in range(nb)], axis=0) for half in (0, 1)]
        vrows = slice(h * V_ROWS, (h + 1) * V_ROWS)
        acc = _dot(vc_ref[0, 0, vrows, :], p_half[1]) + _dot(vp_ref[0, 0, vrows, :], p_half[0])
        o_ref[0, h * HEAD_DIM:(h + 1) * HEAD_DIM, :] = (
            acc[:HEAD_DIM] / acc[HEAD_DIM:HEAD_DIM + 1]).astype(o_ref.dtype)


def _chunk(kc, qc, vc, blocks):
    B, nk, tk, hc = kc.shape
    S = qc.shape[2]
    hv = vc.shape[2]
    tq = tk
    assert tq == LEFT_CHUNKS * CHUNK and 2 * CHUNK == LANES
    prev = lambda i: jnp.maximum(i - 1, 0)
    return pl.pallas_call(
        _chunk_kernel,
        grid=(B, S // tq),
        in_specs=[
            pl.BlockSpec((1, hc, tq), lambda b, i: (b, 0, i)),
            pl.BlockSpec((1, 1, tk, hc), lambda b, i: (b, prev(i), 0, 0)),
            pl.BlockSpec((1, 1, tk, hc), lambda b, i: (b, i, 0, 0)),
            pl.BlockSpec((1, 1, hv, tk), lambda b, i: (b, prev(i), 0, 0)),
            pl.BlockSpec((1, 1, hv, tk), lambda b, i: (b, i, 0, 0)),
            _const_spec(blocks.shape),
        ],
        out_specs=pl.BlockSpec((1, hc, tq), lambda b, i: (b, 0, i)),
        out_shape=jax.ShapeDtypeStruct((B, hc, S), BF16),
        compiler_params=_params(("arbitrary", "arbitrary")),
        name="chunk",
    )(qc, kc, kc, vc, vc, blocks)


def _chunk_bias_blocks(rel_bias):
    heads = rel_bias.shape[0]
    n = LANES
    span = 3 * n - 1
    dist = np.arange(span) - (n - 1)
    by_dist = rel_bias.astype(F32)[:, np.clip(dist, -(CHUNK - 1), REL_CLIP) + (CHUNK - 1)] * LOG2E
    padded = jnp.pad(by_dist, ((0, 0), (0, 1)))
    skew = jnp.broadcast_to(padded[:, None, :], (heads, n, span + 1)).reshape(heads, n * (span + 1))
    near = skew[:, :n * span].reshape(heads, n, span)[:, :, n - 1:]
    key_chunk = (np.arange(n) // CHUNK)[:, None]
    qry_chunk = (np.arange(n) // CHUNK)[None, :]
    clipped = jnp.broadcast_to(by_dist[:, -1][:, None, None], (heads, n, n))
    blocks = [
        jnp.where((qry_chunk >= key_chunk)[None], near[:, :, :n], NEG_INF),
        near[:, :, n:],
        clipped,
        jnp.where((2 * BAND_BLOCKS + qry_chunk - key_chunk <= LEFT_CHUNKS)[None], clipped, NEG_INF),
    ]
    return jnp.stack(blocks, axis=1)


def _memkv_kernel(mem_ref, g_ref, w_ref, kv_ref):
    m = _rms_rows(mem_ref[0], g_ref[...]).astype(BF16)
    kv_ref[0] = _dot(m, w_ref[...]).astype(BF16)


def _memkv(mem, g, w):
    B, N, D = mem.shape
    return pl.pallas_call(
        _memkv_kernel,
        grid=(B,),
        in_specs=[pl.BlockSpec((1, N, D), lambda b: (b, 0, 0)), _const_spec(g.shape), _const_spec(w.shape)],
        out_specs=pl.BlockSpec((1, N, 2 * D), lambda b: (b, 0, 0)),
        out_shape=jax.ShapeDtypeStruct((B, N, 2 * D), BF16),
        compiler_params=_params(("arbitrary",)),
        name="memkv",
    )(mem, g, w)


def _mix_mem_kernel(x_ref, of_ref, oc_ref, kv_ref, gfo_ref, gco_ref, wout_ref, gmp_ref,
                    gmq_ref, wq_ref, wo_ref, gmo_ref, gpre_ref, wg_ref, wu_ref, wd_ref, gpost_ref, y_ref, act_ref):
    tm, d = x_ref.shape[1:]
    dh = d // MEM_HEADS
    kv = kv_ref[0]
    groups = [slice(r * tm // MIX_GROUPS, (r + 1) * tm // MIX_GROUPS) for r in range(MIX_GROUPS)]

    def mixer_input(rows):
        of = jnp.transpose(of_ref[0, :, rows].astype(F32))
        oc = jnp.transpose(oc_ref[0, :, rows].astype(F32))
        return jnp.concatenate([_rms_rows(of, gfo_ref[...]), _rms_rows(oc, gco_ref[...])], axis=-1).astype(BF16)

    def cross_attention(q):
        outs = []
        for hh in range(MEM_HEADS):
            sl = slice(hh * dh, (hh + 1) * dh)
            s = _dot_nt(q[:, sl], kv[:, sl])
            p = jnp.exp2(s - jnp.max(s, axis=-1, keepdims=True))
            l = jnp.sum(p, axis=-1, keepdims=True)
            outs.append(_dot(p.astype(BF16), kv[:, d + hh * dh:d + (hh + 1) * dh]) / l)
        return jnp.concatenate(outs, axis=-1).astype(BF16)

    cat = [mixer_input(rows) for rows in groups]
    x1 = [x_ref[0, rows, :] + _rms_rows(_dot(c, wout_ref[...]), gmp_ref[...]) for rows, c in zip(groups, cat)]
    q = [_dot(_rms_rows(v, gmq_ref[...]).astype(BF16), wq_ref[...]).astype(BF16) for v in x1]
    ca = [cross_attention(v) for v in q]
    x2 = [v + _rms_rows(_dot(c, wo_ref[...]), gmo_ref[...]) for v, c in zip(x1, ca)]
    dff = wg_ref.shape[1]
    hf = [_rms_rows(v, gpre_ref[...]).astype(BF16) for v in x2]
    for rows, hr in zip(groups, hf):
        for c0 in range(0, dff, FFN_CHUNK):
            c1 = min(c0 + FFN_CHUNK, dff)
            act_ref[rows, c0:c1] = (jax.nn.silu(_dot(hr, wg_ref[:, c0:c1])) * _dot(hr, wu_ref[:, c0:c1])).astype(BF16)
    down = [_dot(act_ref[rows, :], wd_ref[...]) for rows in groups]
    for rows, v, dn in zip(groups, x2, down):
        y_ref[0, rows, :] = v + _rms_rows(dn, gpost_ref[...])


def _mix_mem(x, of, oc, kv, wp):
    B, S, D = x.shape
    tm = TOKEN_TILE
    half = of.shape[1]
    weights = [wp["g_fox_out"], wp["g_chunk_out"], wp["w_out"], wp["g_mix_post"],
               wp["g_mem_pre"], wp["w_mem_q"], wp["w_mem_o"], wp["g_mem_post"],
               wp["g_ffn_pre"], wp["w_gate"], wp["w_up"], wp["w_down"], wp["g_ffn_post"]]
    return pl.pallas_call(
        _mix_mem_kernel,
        grid=(B, S // tm),
        in_specs=[
            pl.BlockSpec((1, tm, D), lambda b, i: (b, i, 0)),
            pl.BlockSpec((1, half, tm), lambda b, i: (b, 0, i)),
            pl.BlockSpec((1, half, tm), lambda b, i: (b, 0, i)),
            pl.BlockSpec((1,) + kv.shape[1:], lambda b, i: (b, 0, 0)),
        ] + [_const_spec(w.shape) for w in weights],
        out_specs=pl.BlockSpec((1, tm, D), lambda b, i: (b, i, 0)),
        out_shape=jax.ShapeDtypeStruct((B, S, D), F32),
        scratch_shapes=[pltpu.VMEM((tm, wp["w_gate"].shape[1]), BF16)],
        compiler_params=_params(("arbitrary", "arbitrary")),
        name="mix_mem",
    )(x, of, oc, kv, *weights)


FFN_CHUNK = 1024
FFN_GROUPS = 2


def _ffn_kernel(x_ref, gpre_ref, wg_ref, wu_ref, wd_ref, gpost_ref, y_ref, act_ref):
    tm = x_ref.shape[1]
    dff = wg_ref.shape[1]
    groups = [slice(r * tm // FFN_GROUPS, (r + 1) * tm // FFN_GROUPS) for r in range(FFN_GROUPS)]
    h = [_rms_rows(x_ref[0, rows, :], gpre_ref[...]).astype(BF16) for rows in groups]
    for rows, hr in zip(groups, h):
        for c0 in range(0, dff, FFN_CHUNK):
            c1 = min(c0 + FFN_CHUNK, dff)
            g = _dot(hr, wg_ref[:, c0:c1])
            u = _dot(hr, wu_ref[:, c0:c1])
            act_ref[rows, c0:c1] = (jax.nn.silu(g) * u).astype(BF16)
    down = [_dot(act_ref[rows, :], wd_ref[...]) for rows in groups]
    for rows, dn in zip(groups, down):
        y_ref[0, rows, :] = x_ref[0, rows, :] + _rms_rows(dn, gpost_ref[...])


def _ffn(x, wp):
    B, S, D = x.shape
    tm = FFN_GROUPS * TOKEN_TILE
    weights = [wp["g_ffn_pre"], wp["w_gate"], wp["w_up"], wp["w_down"], wp["g_ffn_post"]]
    dff = wp["w_gate"].shape[1]
    return pl.pallas_call(
        _ffn_kernel,
        grid=(B, S // tm),
        in_specs=[pl.BlockSpec((1, tm, D), lambda b, i: (b, i, 0))] + [_const_spec(w.shape) for w in weights],
        out_specs=pl.BlockSpec((1, tm, D), lambda b, i: (b, i, 0)),
        out_shape=jax.ShapeDtypeStruct((B, S, D), F32),
        scratch_shapes=[pltpu.VMEM((tm, dff), BF16)],
        compiler_params=_params(("arbitrary", "arbitrary")),
        name="ffn",
    )(x, *weights)


def _pad_heads_cols(w, heads, width):
    d = w.shape[0]
    w = w.reshape(d, heads, HEAD_DIM)
    w = jnp.pad(w, ((0, 0), (0, 0), (0, width - HEAD_DIM)))
    return w.reshape(d, heads * width)


def _placement(entries, rows, cols):
    m = np.zeros((rows, cols), np.float32)
    for r, c, v in entries:
        m[r, c] = v
    return jnp.asarray(m, BF16)


def _prep_layer(l, g_mix_pre, w_in, b_fgate, g_fox_out, g_chunk_out, w_out, g_mix_post, g_mem_pre, g_mem_kv,
                w_mem_q, w_mem_kv, w_mem_o, g_mem_post, g_ffn_pre, w_gate_up, w_down, g_ffn_post):
    d = w_in.shape[1]
    fw = FOX_HEADS * HEAD_DIM
    cw = CHUNK_HEADS * HEAD_DIM
    w = w_in[l]
    o = 0
    wq_f, wk_f, wv_f = w[:, o:o + fw], w[:, o + fw:o + 2 * fw], w[:, o + 2 * fw:o + 3 * fw]
    o += 3 * fw
    w_f = w[:, o:o + FOX_HEADS]
    o += FOX_HEADS
    wq_c, wk_c, wv_c = w[:, o:o + cw], w[:, o + cw:o + 2 * cw], w[:, o + 2 * cw:o + 3 * cw]
    scale = HEAD_DIM ** -0.5 * LOG2E

    row = lambda v: v.astype(F32)[None, :]
    wf3 = jnp.pad(jnp.tile(w_f, (1, N_SPLIT)), ((0, 0), (0, LANES - N_SPLIT * FOX_HEADS)))
    bf3 = jnp.pad(jnp.tile(b_fgate[l].astype(F32), N_SPLIT), (0, LANES - N_SPLIT * FOX_HEADS))[None, :]

    ek, eq = [], []
    for h in range(FOX_HEADS):
        for i in range(N_SPLIT):
            gate_lane = (h // 2) * HEAD_PAD + (HEAD_DIM if h % 2 == 0 else 0)
            ek.append((ONES_LANE, gate_lane + i, 1.0))
            ek.append((i * FOX_HEADS + h, gate_lane + N_SPLIT + i, -1.0))
            eq.append((h * Q_ROWS + HEAD_DIM + i, i * FOX_HEADS + h, 1.0))
            eq.append((h * Q_ROWS + HEAD_DIM + N_SPLIT + i, ONES_LANE, 1.0))
    gk = [(h * HEAD_DIM + d, h, 1.0) for h in range(FOX_HEADS) for d in range(HEAD_DIM)]
    gq = [(h, h * Q_ROWS + d, 1.0) for h in range(FOX_HEADS) for d in range(HEAD_DIM)]
    ones_rows = np.zeros((FOX_HEADS * V_ROWS, LANES), np.float32)
    ones_rows[np.arange(FOX_HEADS) * V_ROWS + HEAD_DIM] = 1.0

    dff = w_gate_up.shape[2] // 2
    mem_scale = (w_mem_q.shape[2] // MEM_HEADS) ** -0.5 * LOG2E
    return dict(
        g_mix_pre=row(g_mix_pre[l]),
        wf=wf3.astype(BF16), bf=bf3,
        wkf=wk_f.astype(BF16),
        wqf=_pad_heads_cols(wq_f * scale, FOX_HEADS, Q_ROWS).T.astype(BF16),
        wvf=_pad_heads_cols(wv_f, FOX_HEADS, V_ROWS).T.astype(BF16),
        wkc=wk_c.astype(BF16),
        wqc=(wq_c * scale).T.astype(BF16),
        wvc=_pad_heads_cols(wv_c, CHUNK_HEADS, V_ROWS).T.astype(BF16),
        ek=_placement(ek, LANES, FOX_HEADS * HEAD_DIM),
        eq=_placement(eq, FOX_HEADS * Q_ROWS, LANES),
        ones_rows=jnp.asarray(ones_rows),
        gk=_placement(gk, FOX_HEADS * HEAD_DIM, LANES),
        gq=_placement(gq, 2 * FOX_HEADS, FOX_HEADS * Q_ROWS),
        g_fox_out=row(g_fox_out[l]), g_chunk_out=row(g_chunk_out[l]),
        w_out=w_out[l].astype(BF16), g_mix_post=row(g_mix_post[l]),
        g_mem_pre=row(g_mem_pre[l]), g_mem_kv=row(g_mem_kv[l]),
        w_mem_q=(w_mem_q[l] * mem_scale).astype(BF16), w_mem_kv=w_mem_kv[l].astype(BF16),
        w_mem_o=w_mem_o[l].astype(BF16), g_mem_post=row(g_mem_post[l]),
        g_ffn_pre=row(g_ffn_pre[l]),
        w_gate=w_gate_up[l][:, :dff].astype(BF16), w_up=w_gate_up[l][:, dff:].astype(BF16),
        w_down=w_down[l].astype(BF16), g_ffn_post=row(g_ffn_post[l]),
    )


def kernel(x, mem, g_mix_pre, w_in, b_fgate, rel_bias, g_fox_out, g_chunk_out, w_out, g_mix_post, g_mem_pre,
           g_mem_kv, w_mem_q, w_mem_kv, w_mem_o, g_mem_post, g_ffn_pre, w_gate_up, w_down, g_ffn_post):
    depth = w_in.shape[0]
    for l in range(depth):
        wp = _prep_layer(l, g_mix_pre, w_in, b_fgate, g_fox_out, g_chunk_out, w_out, g_mix_post, g_mem_pre,
                         g_mem_kv, w_mem_q, w_mem_kv, w_mem_o, g_mem_post, g_ffn_pre, w_gate_up, w_down,
                         g_ffn_post)
        kf, qf, vf, kc, qc, vc, kstats, qstats, q_norm2, c_rows = _inproj(x, wp["g_mix_pre"], wp)
        o_f = _fox(kf, qf, vf, kstats, qstats, q_norm2, c_rows)
        o_c = _chunk(kc, qc, vc, _chunk_bias_blocks(rel_bias[l]))
        kv = _memkv(mem, wp["g_mem_kv"], wp["w_mem_kv"])
        x = _mix_mem(x, o_f, o_c, kv, wp)
    return x
```
